```python
import jax, jax.numpy as jnp
from jax import lax
import numpy as np

D_MODEL = 1024
BATCH = 32
SEQ = 2048
DEPTH = 1

CHUNK = 64
EXPAND = 2
E_TOTAL = EXPAND * D_MODEL
E_POOL = E_TOTAL // 2
E_CONV = E_TOTAL // 2
POOL_WINDOWS = (2, 4, 8, 16)
N_POOL_GROUPS = len(POOL_WINDOWS)
POOL_GROUP = E_POOL // N_POOL_GROUPS
CONV_K = 3
N_BRANCHES = 2
RMS_EPS = 1e-6

IN_SPLITS = (E_POOL, E_POOL,
             E_CONV, E_CONV, E_CONV, E_CONV,
             D_MODEL, D_MODEL)
IN_WIDTH = sum(IN_SPLITS)

kernel_name = "hybrid_pool_shortconv_gated_merge_adaln"


def rmsnorm(x, g):
    xf = x.astype(jnp.float32)
    r = lax.rsqrt(jnp.mean(xf * xf, axis=-1, keepdims=True) + RMS_EPS)
    return (xf * r).astype(x.dtype) * g


def split_cols(z):
    idx = np.cumsum(IN_SPLITS)[:-1].tolist()
    return jnp.split(z, idx, axis=-1)


def multiscale_pool_residual(u, pool_w):
    b, s, _ = u.shape
    uf = u.astype(jnp.float32)
    cs = jnp.pad(jnp.cumsum(uf, axis=1), ((0, 0), (1, 0), (0, 0)))
    pos = jnp.arange(1, s + 1, dtype=jnp.float32)
    groups = []
    for gi, w in enumerate(POOL_WINDOWS):
        sl = slice(gi * POOL_GROUP, (gi + 1) * POOL_GROUP)
        c_g = cs[:, :, sl]
        lagged = jnp.pad(c_g, ((0, 0), (w, 0), (0, 0)))[:, : s + 1]
        wsum = c_g[:, 1:] - lagged[:, 1:]
        cnt = jnp.minimum(pos, float(w))[None, :, None]
        groups.append(wsum / cnt - uf[:, :, sl])
    pooled = jnp.stack(groups, axis=2)
    mixed = jnp.einsum("bsgi,gio->bsgo", pooled, pool_w.astype(jnp.float32))
    return mixed.reshape(b, s, E_POOL).astype(u.dtype)


def causal_depthwise_conv(v, k, bias):
    s = v.shape[1]
    vp = jnp.pad(v, ((0, 0), (CONV_K - 1, 0), (0, 0)))
    out = bias
    for j in range(CONV_K):
        out = out + vp[:, j:j + s] * k[j]
    return out


def setup_inputs(seed: int = 0) -> dict:
    key = jax.random.key(seed)
    ks = jax.random.split(key, 16)
    f32 = jnp.float32
    nrm = lambda k, shape, sc: jax.random.normal(k, shape, f32) * sc
    return {
        "x": nrm(ks[0], (BATCH, SEQ, D_MODEL), 1.0),
        "c": nrm(ks[1], (BATCH, D_MODEL), 1.0),
        "ada_w": nrm(ks[2], (DEPTH, D_MODEL, 3 * D_MODEL), 0.2 * D_MODEL ** -0.5),
        "ada_b": nrm(ks[3], (DEPTH, 3 * D_MODEL), 0.02),
        "norm_g": 1.0 + nrm(ks[4], (DEPTH, D_MODEL), 0.05),
        "w_in": nrm(ks[5], (DEPTH, D_MODEL, IN_WIDTH), D_MODEL ** -0.5),
        "b_in": nrm(ks[6], (DEPTH, IN_WIDTH), 0.02),
        "pool_w": nrm(ks[7], (DEPTH, N_POOL_GROUPS, POOL_GROUP, POOL_GROUP), POOL_GROUP ** -0.5),
        "pool_scale": 1.0 + nrm(ks[8], (DEPTH, E_POOL), 0.1),
        "conv_w": nrm(ks[9], (DEPTH, CONV_K, E_CONV), CONV_K ** -0.5),
        "conv_b": nrm(ks[10], (DEPTH, E_CONV), 0.02),
        "w_out_a": nrm(ks[11], (DEPTH, E_POOL, D_MODEL), E_POOL ** -0.5),
        "w_out_b": nrm(ks[12], (DEPTH, E_CONV, D_MODEL), E_CONV ** -0.5),
        "w_o": nrm(ks[13], (DEPTH, D_MODEL, D_MODEL), D_MODEL ** -0.5),
        "final_g": 1.0 + nrm(ks[14], (D_MODEL,), 0.05),
    }


def reference(x, c, ada_w, ada_b, norm_g, w_in, b_in, pool_w, pool_scale,
              conv_w, conv_b, w_out_a, w_out_b, w_o, final_g):
    c_act = jax.nn.silu(c)
    for l in range(DEPTH):
        mod = c_act @ ada_w[l] + ada_b[l]
        shift, scale, gate = jnp.split(mod, 3, axis=-1)
        h = rmsnorm(x, norm_g[l]) * (1.0 + scale[:, None, :]) + shift[:, None, :]

        z = h @ w_in[l] + b_in[l]
        a_v, a_g, b_B, b_C, b_v, b_g, m_a, m_b = split_cols(z)

        y_a = multiscale_pool_residual(a_v, pool_w[l]) * pool_scale[l] * jax.nn.silu(a_g)
        o_a = y_a @ w_out_a[l]

        y_b = b_B * causal_depthwise_conv(b_C * b_v, conv_w[l], conv_b[l]) * jax.nn.silu(b_g)
        o_b = y_b @ w_out_b[l]

        merged = jax.nn.sigmoid(m_a) * o_a + jax.nn.sigmoid(m_b) * o_b
        x = x + gate[:, None, :] * (merged @ w_o[l])
    return rmsnorm(x, final_g)
```

```python
import functools

import jax
import jax.numpy as jnp
from jax import lax
from jax.experimental import pallas as pl
from jax.experimental.pallas import tpu as pltpu

POOL_WINDOWS = (2, 4, 8, 16)
CONV_K = 3
RMS_EPS = 1e-6

SEQ_TILE = 256
POOL_HALO = 16
CONV_HALO = 8
LANES = 128
VMEM_LIMIT_BYTES = 52 * 1024 * 1024

A_V, A_G, B_B, B_C, B_V, B_G, M_A, M_B = range(8)


def _silu(v):
    return v * jax.nn.sigmoid(v)


def _rms_scale(v):
    return lax.rsqrt(jnp.mean(v * v, axis=-1, keepdims=True) + RMS_EPS)


def _ada_kernel(c_ref, w_ref, b_ref, o_ref):
    c_act = _silu(c_ref[...])
    o_ref[...] = jnp.dot(c_act.astype(jnp.bfloat16), w_ref[...].astype(jnp.bfloat16),
                         preferred_element_type=jnp.float32) + b_ref[...]


def _ada_modulation(c, ada_w, ada_b):
    b, d = c.shape
    n = ada_w.shape[1]
    return pl.pallas_call(
        _ada_kernel,
        grid=(n // d,),
        in_specs=[pl.BlockSpec((b, d), lambda j: (0, 0)),
                  pl.BlockSpec((d, d), lambda j: (0, j)),
                  pl.BlockSpec((1, d), lambda j: (0, j))],
        out_specs=pl.BlockSpec((b, d), lambda j: (0, j)),
        out_shape=jax.ShapeDtypeStruct((b, n), jnp.float32),
        compiler_params=pltpu.CompilerParams(dimension_semantics=("arbitrary",)),
        name="ada_modulation",
    )(c, ada_w, ada_b.reshape(1, n))


def _block_kernel(x_ref, mod_ref, norm_g_ref, b_in_ref, pool_scale_ref, conv_w_ref, conv_b_ref,
                  final_g_ref, w_in_ref, pool_w_ref, w_out_a_ref, w_out_b_ref, w_o_ref,
                  o_ref, av_ref, u_ref, *, final_norm):
    ts = x_ref.shape[1]
    e = w_out_a_ref.shape[0]
    pg = pool_w_ref.shape[1]
    s = pl.program_id(1)

    @pl.when(s == 0)
    def _():
        av_ref[0:POOL_HALO, :] = jnp.zeros((POOL_HALO, e), jnp.float32)
        u_ref[0:CONV_HALO, :] = jnp.zeros((CONV_HALO, e), jnp.float32)

    x = x_ref[0]
    shift = mod_ref[0, 0:1, :]
    scale = mod_ref[0, 1:2, :]
    gate = mod_ref[0, 2:3, :]
    h = (x * _rms_scale(x)) * norm_g_ref[...] * (1.0 + scale) + shift
    hb = h.astype(jnp.bfloat16)

    def proj(k):
        cols = slice(k * e, (k + 1) * e)
        return jnp.dot(hb, w_in_ref[:, cols], preferred_element_type=jnp.float32) + b_in_ref[:, cols]

    a_v = proj(A_V)
    av_ref[POOL_HALO:, :] = a_v
    a_ext = av_ref[...]
    av_ref[0:POOL_HALO, :] = a_v[ts - POOL_HALO:, :]
    frames_seen = (s * ts + 1 + lax.broadcasted_iota(jnp.int32, (ts, LANES), 0)).astype(jnp.float32)
    mixed = []
    for gi, w in enumerate(POOL_WINDOWS):
        grp = a_ext[:, gi * pg:(gi + 1) * pg]
        wsum = grp
        sh = 1
        while sh < w:
            wsum = wsum + pltpu.roll(wsum, sh, 0)
            sh *= 2
        inv_cnt = 1.0 / jnp.minimum(frames_seen, float(w))
        inv_cnt = jnp.concatenate([inv_cnt] * (pg // LANES), axis=1)
        pooled = wsum[POOL_HALO:, :] * inv_cnt - grp[POOL_HALO:, :]
        mixed.append(jnp.dot(pooled.astype(jnp.bfloat16), pool_w_ref[gi],
                             preferred_element_type=jnp.float32))
    mixed = jnp.concatenate(mixed, axis=1)
    y_a = mixed * pool_scale_ref[...] * _silu(proj(A_G))
    o_a = jnp.dot(y_a.astype(jnp.bfloat16), w_out_a_ref[...], preferred_element_type=jnp.float32)
    merged = jax.nn.sigmoid(proj(M_A)) * o_a

    u = proj(B_C) * proj(B_V)
    u_ref[CONV_HALO:, :] = u
    u_ext = u_ref[...]
    u_ref[0:CONV_HALO, :] = u[ts - CONV_HALO:, :]
    conv = conv_b_ref[...]
    for j in range(CONV_K):
        lag = CONV_K - 1 - j
        tap = u if lag == 0 else pltpu.roll(u_ext, lag, 0)[CONV_HALO:, :]
        conv = conv + tap * conv_w_ref[j:j + 1, :]
    y_b = proj(B_B) * conv * _silu(proj(B_G))
    o_b = jnp.dot(y_b.astype(jnp.bfloat16), w_out_b_ref[...], preferred_element_type=jnp.float32)
    merged = merged + jax.nn.sigmoid(proj(M_B)) * o_b

    out = jnp.dot(merged.astype(jnp.bfloat16), w_o_ref[...], preferred_element_type=jnp.float32)
    x_new = x + gate * out
    if final_norm:
        x_new = (x_new * _rms_scale(x_new)) * final_g_ref[...]
    o_ref[0] = x_new


def _resident(shape):
    return pl.BlockSpec(shape, lambda b, s: (0,) * len(shape), pipeline_mode=pl.Buffered(1))


def _block_layer(x, mod, norm_g, b_in, pool_scale, conv_w, conv_b, final_g,
                 w_in, pool_w, w_out_a, w_out_b, w_o, *, final_norm):
    b, seq, d = x.shape
    e = w_out_a.shape[0]
    ts = SEQ_TILE
    assert seq % ts == 0 and ts >= POOL_HALO >= max(POOL_WINDOWS) - 1 and CONV_HALO >= CONV_K - 1
    assert w_in.shape == (d, 8 * e) and pool_w.shape[0] == len(POOL_WINDOWS)
    row = lambda v: v.reshape(1, -1)
    small = [row(norm_g), row(b_in), row(pool_scale), conv_w, row(conv_b), row(final_g)]
    weights = [w.astype(jnp.bfloat16) for w in (w_in, pool_w, w_out_a, w_out_b, w_o)]
    return pl.pallas_call(
        functools.partial(_block_kernel, final_norm=final_norm),
        grid=(b, seq // ts),
        in_specs=[pl.BlockSpec((1, ts, d), lambda i, s: (i, s, 0)),
                  pl.BlockSpec((1, 3, d), lambda i, s: (i, 0, 0))]
                 + [_resident(v.shape) for v in small]
                 + [_resident(w.shape) for w in weights],
        out_specs=pl.BlockSpec((1, ts, d), lambda i, s: (i, s, 0)),
        out_shape=jax.ShapeDtypeStruct(x.shape, x.dtype),
        scratch_shapes=[pltpu.VMEM((POOL_HALO + ts, e), jnp.float32),
                        pltpu.VMEM((CONV_HALO + ts, e), jnp.float32)],
        compiler_params=pltpu.CompilerParams(
            dimension_semantics=("arbitrary", "arbitrary"),
            vmem_limit_bytes=VMEM_LIMIT_BYTES),
        name="fused_block",
    )(x, mod, *small, *weights)


def kernel(x, c, ada_w, ada_b, norm_g, w_in, b_in, pool_w, pool_scale, conv_w, conv_b,
           w_out_a, w_out_b, w_o, final_g):
    depth = ada_w.shape[0]
    b, _, d = x.shape
    for l in range(depth):
        mod = _ada_modulation(c, ada_w[l], ada_b[l]).reshape(b, 3, d)
        x = _block_layer(x, mod, norm_g[l], b_in[l], pool_scale[l], conv_w[l], conv_b[l], final_g,
                         w_in[l], pool_w[l], w_out_a[l], w_out_b[l], w_o[l],
                         final_norm=(l == depth - 1))
    return x
```

```python
import functools

import jax
import jax.numpy as jnp
from jax import lax
from jax.experimental import pallas as pl
from jax.experimental.pallas import tpu as pltpu

POOL_WINDOWS = (2, 4, 8, 16)
CONV_K = 3
RMS_EPS = 1e-6

SEQ_TILE = 512
ROW_CHUNK = 256
COL_TILE = 512
POOL_HALO = 16
CONV_HALO = 8
LANES = 128
VMEM_LIMIT_BYTES = 52 * 1024 * 1024

A_V, A_G, B_B, B_C, B_V, B_G, M_A, M_B = range(8)


def _silu(v):
    return v * jax.nn.sigmoid(v)


def _rms_scale(v):
    return lax.rsqrt(jnp.mean(v * v, axis=-1, keepdims=True) + RMS_EPS)


def _ada_kernel(c_ref, w_ref, b_ref, o_ref):
    c_act = _silu(c_ref[...])
    o_ref[...] = jnp.dot(c_act.astype(jnp.bfloat16), w_ref[...].astype(jnp.bfloat16),
                         preferred_element_type=jnp.float32) + b_ref[...]


def _ada_modulation(c, ada_w, ada_b):
    b, d = c.shape
    n = ada_w.shape[1]
    return pl.pallas_call(
        _ada_kernel,
        grid=(n // d,),
        in_specs=[pl.BlockSpec((b, d), lambda j: (0, 0)),
                  pl.BlockSpec((d, d), lambda j: (0, j)),
                  pl.BlockSpec((1, d), lambda j: (0, j))],
        out_specs=pl.BlockSpec((b, d), lambda j: (0, j)),
        out_shape=jax.ShapeDtypeStruct((b, n), jnp.float32),
        compiler_params=pltpu.CompilerParams(dimension_semantics=("arbitrary",)),
        name="ada_modulation",
    )(c, ada_w, ada_b.reshape(1, n))


def _block_kernel(x_ref, mod_ref, norm_g_ref, b_in_ref, pool_scale_ref, conv_w_ref, conv_b_ref,
                  final_g_ref, w_in_ref, pool_w_ref, w_out_a_ref, w_out_b_ref, w_o_ref,
                  o_ref, av_ref, u_ref, *, final_norm):
    ts = x_ref.shape[1]
    e = w_out_a_ref.shape[0]
    pg = pool_w_ref.shape[1]
    rc = ROW_CHUNK
    s = pl.program_id(1)

    @pl.when(s == 0)
    def _():
        av_ref[0:POOL_HALO, :] = jnp.zeros((POOL_HALO, e), jnp.float32)
        u_ref[0:CONV_HALO, :] = jnp.zeros((CONV_HALO, e), jnp.float32)

    shift = mod_ref[0, 0:1, :]
    scale = mod_ref[0, 1:2, :]
    gate = mod_ref[0, 2:3, :]

    def make_chunk(c):
        r0 = c * rc
        v = {}

        def tile_cols(h, k=0):
            return slice(k * e + h * COL_TILE, k * e + (h + 1) * COL_TILE)

        def proj(k, h):
            cols = tile_cols(h, k)
            return (jnp.dot(v["hb"], w_in_ref[:, cols], preferred_element_type=jnp.float32)
                    + b_in_ref[:, cols])

        def wide(tile_fn):
            return jnp.concatenate([tile_fn(h) for h in range(e // COL_TILE)], axis=1)

        def front():
            x = x_ref[0, r0:r0 + rc, :]
            h = (x * _rms_scale(x)) * norm_g_ref[...] * (1.0 + scale) + shift
            v["x"] = x
            v["hb"] = h.astype(jnp.bfloat16)

        def pool_value():
            for h in range(e // COL_TILE):
                av_ref[POOL_HALO + r0:POOL_HALO + r0 + rc, tile_cols(h)] = proj(A_V, h)

        def conv_value():
            def tile(h):
                cols = tile_cols(h)
                u = proj(B_C, h) * proj(B_V, h)
                u_ref[CONV_HALO + r0:CONV_HALO + r0 + rc, cols] = u
                u_ext = u_ref[r0:r0 + CONV_HALO + rc, cols]
                conv = conv_b_ref[:, cols]
                for j in range(CONV_K):
                    lag = CONV_K - 1 - j
                    tap = u if lag == 0 else pltpu.roll(u_ext, lag, 0)[CONV_HALO:, :]
                    conv = conv + tap * conv_w_ref[j:j + 1, cols]
                return conv
            v["conv"] = [tile(h) for h in range(e // COL_TILE)]

        def pool_gate():
            v["silu_a"] = wide(lambda h: _silu(proj(A_G, h)))

        def pool_mix():
            a_ext = av_ref[r0:r0 + POOL_HALO + rc, :]
            frames_seen = (s * ts + r0 + 1
                           + lax.broadcasted_iota(jnp.int32, (rc, LANES), 0)).astype(jnp.float32)
            mixed = []
            for gi, w in enumerate(POOL_WINDOWS):
                grp = a_ext[:, gi * pg:(gi + 1) * pg]
                wsum = grp
                sh = 1
                while sh < w:
                    wsum = wsum + pltpu.roll(wsum, sh, 0)
                    sh *= 2
                inv_cnt = 1.0 / jnp.minimum(frames_seen, float(w))
                inv_cnt = jnp.concatenate([inv_cnt] * (pg // LANES), axis=1)
                pooled = wsum[POOL_HALO:, :] * inv_cnt - grp[POOL_HALO:, :]
                mixed.append(jnp.dot(pooled.astype(jnp.bfloat16), pool_w_ref[gi],
                                     preferred_element_type=jnp.float32))
            mixed = jnp.concatenate(mixed, axis=1)
            v["y_a"] = (mixed * pool_scale_ref[...] * v["silu_a"]).astype(jnp.bfloat16)

        def conv_gates():
            v["y_b"] = wide(lambda h: (proj(B_B, h) * v["conv"][h]
                                       * _silu(proj(B_G, h))).astype(jnp.bfloat16))

        def merge_a():
            def tile(h):
                o_a = jnp.dot(v["y_a"], w_out_a_ref[:, tile_cols(h)],
                              preferred_element_type=jnp.float32)
                return jax.nn.sigmoid(proj(M_A, h)) * o_a
            v["merged"] = [tile(h) for h in range(e // COL_TILE)]

        def merge_b():
            def tile(h):
                o_b = jnp.dot(v["y_b"], w_out_b_ref[:, tile_cols(h)],
                              preferred_element_type=jnp.float32)
                return (v["merged"][h] + jax.nn.sigmoid(proj(M_B, h)) * o_b).astype(jnp.bfloat16)
            v["merged"] = wide(tile)

        def output():
            out = wide(lambda h: jnp.dot(v["merged"], w_o_ref[:, tile_cols(h)],
                                         preferred_element_type=jnp.float32))
            x_new = v["x"] + gate * out
            if final_norm:
                x_new = (x_new * _rms_scale(x_new)) * final_g_ref[...]
            o_ref[0, r0:r0 + rc, :] = x_new

        head = [front, pool_value, conv_value]
        body = [pool_gate, pool_mix, conv_gates, merge_a, merge_b]
        return head, body, output

    pending_output = None
    for head, body, output in [make_chunk(c) for c in range(ts // rc)]:
        for stage in head:
            stage()
        if pending_output is not None:
            pending_output()
        for stage in body:
            stage()
        pending_output = output
    pending_output()

    av_ref[0:POOL_HALO, :] = av_ref[ts:ts + POOL_HALO, :]
    u_ref[0:CONV_HALO, :] = u_ref[ts:ts + CONV_HALO, :]


def _resident(shape):
    return pl.BlockSpec(shape, lambda b, s: (0,) * len(shape), pipeline_mode=pl.Buffered(1))


def _block_layer(x, mod, norm_g, b_in, pool_scale, conv_w, conv_b, final_g,
                 w_in, pool_w, w_out_a, w_out_b, w_o, *, final_norm):
    b, seq, d = x.shape
    e = w_out_a.shape[0]
    ts = SEQ_TILE
    assert seq % ts == 0 and ts % ROW_CHUNK == 0 and ROW_CHUNK >= POOL_HALO >= max(POOL_WINDOWS) - 1
    assert CONV_HALO >= CONV_K - 1
    assert w_in.shape == (d, 8 * e) and pool_w.shape[0] == len(POOL_WINDOWS)
    row = lambda v: v.reshape(1, -1)
    small = [row(norm_g), row(b_in), row(pool_scale), conv_w, row(conv_b), row(final_g)]
    weights = [w.astype(jnp.bfloat16) for w in (w_in, pool_w, w_out_a, w_out_b, w_o)]
    return pl.pallas_call(
        functools.partial(_block_kernel, final_norm=final_norm),
        grid=(b, seq // ts),
        in_specs=[pl.BlockSpec((1, ts, d), lambda i, s: (i, s, 0)),
                  pl.BlockSpec((1, 3, d), lambda i, s: (i, 0, 0))]
                 + [_resident(v.shape) for v in small]
                 + [_resident(w.shape) for w in weights],
        out_specs=pl.BlockSpec((1, ts, d), lambda i, s: (i, s, 0)),
        out_shape=jax.ShapeDtypeStruct(x.shape, x.dtype),
        scratch_shapes=[pltpu.VMEM((POOL_HALO + ts, e), jnp.float32),
                        pltpu.VMEM((CONV_HALO + ts, e), jnp.float32)],
        compiler_params=pltpu.CompilerParams(
            dimension_semantics=("arbitrary", "arbitrary"),
            vmem_limit_bytes=VMEM_LIMIT_BYTES),
        name="fused_block",
    )(x, mod, *small, *weights)


def kernel(x, c, ada_w, ada_b, norm_g, w_in, b_in, pool_w, pool_scale, conv_w, conv_b,
           w_out_a, w_out_b, w_o, final_g):
    depth = ada_w.shape[0]
    b, _, d = x.shape
    for l in range(depth):
        mod = _ada_modulation(c, ada_w[l], ada_b[l]).reshape(b, 3, d)
        x = _block_layer(x, mod, norm_g[l], b_in[l], pool_scale[l], conv_w[l], conv_b[l], final_g,
                         w_in[l], pool_w[l], w_out_a[l], w_out_b[l], w_o[l],
                         final_norm=(l == depth - 1))
    return x
```

```python
import functools

import jax
import jax.numpy as jnp
from jax import lax
from jax.experimental import pallas as pl
from jax.experimental.pallas import tpu as pltpu

POOL_WINDOWS = (2, 4, 8, 16)
CONV_K = 3
RMS_EPS = 1e-6

SEQ_TILE = 512
ROW_CHUNK = 256
COL_TILE = 512
POOL_HALO = 16
CONV_HALO = 8
LANES = 128
VMEM_LIMIT_BYTES = 52 * 1024 * 1024

A_V, A_G, B_B, B_C, B_V, B_G, M_A, M_B = range(8)


def _silu(v):
    return v * jax.nn.sigmoid(v)


def _pack_row_pairs(w):
    wb = w.astype(jnp.bfloat16)
    *lead, k, n = wb.shape
    pairs = jnp.swapaxes(wb.reshape(*lead, k // 2, 2, n), -1, -2)
    return lax.bitcast_convert_type(pairs, jnp.uint32)


def _bf16_rows(packed):
    return pltpu.bitcast(packed, jnp.bfloat16)


def _rms_scale(v):
    return lax.rsqrt(jnp.mean(v * v, axis=-1, keepdims=True) + RMS_EPS)


def _ada_kernel(c_ref, w_ref, b_ref, o_ref):
    c_act = _silu(c_ref[...])
    o_ref[...] = jnp.dot(c_act.astype(jnp.bfloat16), w_ref[...].astype(jnp.bfloat16),
                         preferred_element_type=jnp.float32) + b_ref[...]


def _ada_modulation(c, ada_w, ada_b):
    b, d = c.shape
    n = ada_w.shape[1]
    return pl.pallas_call(
        _ada_kernel,
        grid=(n // d,),
        in_specs=[pl.BlockSpec((b, d), lambda j: (0, 0)),
                  pl.BlockSpec((d, d), lambda j: (0, j)),
                  pl.BlockSpec((1, d), lambda j: (0, j))],
        out_specs=pl.BlockSpec((b, d), lambda j: (0, j)),
        out_shape=jax.ShapeDtypeStruct((b, n), jnp.float32),
        compiler_params=pltpu.CompilerParams(dimension_semantics=("arbitrary",)),
        name="ada_modulation",
    )(c, ada_w, ada_b.reshape(1, n))


def _block_kernel(x_ref, mod_ref, norm_g_ref, b_in_ref, pool_scale_ref, conv_w_ref, conv_b_ref,
                  final_g_ref, w_in_ref, pool_w_ref, w_out_a_ref, w_out_b_ref, w_o_ref,
                  o_ref, av_ref, u_ref, *, final_norm):
    ts = x_ref.shape[1]
    e = w_out_a_ref.shape[1]
    pg = pool_w_ref.shape[2]
    rc = ROW_CHUNK
    s = pl.program_id(1)

    @pl.when(s == 0)
    def _():
        av_ref[0:POOL_HALO, :] = jnp.zeros((POOL_HALO, e), jnp.float32)
        u_ref[0:CONV_HALO, :] = jnp.zeros((CONV_HALO, e), jnp.float32)

    shift = mod_ref[0, 0:1, :]
    scale = mod_ref[0, 1:2, :]
    gate = mod_ref[0, 2:3, :]

    def make_chunk(c):
        r0 = c * rc
        v = {}

        def tile_cols(h, k=0):
            return slice(k * e + h * COL_TILE, k * e + (h + 1) * COL_TILE)

        def proj(k, h):
            cols = tile_cols(h, k)
            return (jnp.dot(v["hb"], _bf16_rows(w_in_ref[:, cols]), preferred_element_type=jnp.float32)
                    + b_in_ref[:, cols])

        def wide(tile_fn):
            return jnp.concatenate([tile_fn(h) for h in range(e // COL_TILE)], axis=1)

        def front():
            x = x_ref[0, r0:r0 + rc, :]
            h = (x * _rms_scale(x)) * norm_g_ref[...] * (1.0 + scale) + shift
            v["x"] = x
            v["hb"] = h.astype(jnp.bfloat16)

        def pool_value():
            for h in range(e // COL_TILE):
                av_ref[POOL_HALO + r0:POOL_HALO + r0 + rc, tile_cols(h)] = proj(A_V, h)

        def conv_value():
            def tile(h):
                cols = tile_cols(h)
                u = proj(B_C, h) * proj(B_V, h)
                u_ref[CONV_HALO + r0:CONV_HALO + r0 + rc, cols] = u
                u_ext = u_ref[r0:r0 + CONV_HALO + rc, cols]
                conv = conv_b_ref[:, cols]
                for j in range(CONV_K):
                    lag = CONV_K - 1 - j
                    tap = u if lag == 0 else pltpu.roll(u_ext, lag, 0)[CONV_HALO:, :]
                    conv = conv + tap * conv_w_ref[j:j + 1, cols]
                return conv
            v["conv"] = [tile(h) for h in range(e // COL_TILE)]

        def pool_gate():
            v["silu_a"] = wide(lambda h: _silu(proj(A_G, h)))

        def pool_mix():
            a_ext = av_ref[r0:r0 + POOL_HALO + rc, :]
            frames_seen = (s * ts + r0 + 1
                           + lax.broadcasted_iota(jnp.int32, (rc, LANES), 0)).astype(jnp.float32)
            mixed = []
            for gi, w in enumerate(POOL_WINDOWS):
                grp = a_ext[:, gi * pg:(gi + 1) * pg]
                wsum = grp
                sh = 1
                while sh < w:
                    wsum = wsum + pltpu.roll(wsum, sh, 0)
                    sh *= 2
                inv_cnt = 1.0 / jnp.minimum(frames_seen, float(w))
                inv_cnt = jnp.concatenate([inv_cnt] * (pg // LANES), axis=1)
                pooled = wsum[POOL_HALO:, :] * inv_cnt - grp[POOL_HALO:, :]
                mixed.append(jnp.dot(pooled.astype(jnp.bfloat16), _bf16_rows(pool_w_ref[gi]),
                                     preferred_element_type=jnp.float32))
            mixed = jnp.concatenate(mixed, axis=1)
            v["y_a"] = (mixed * pool_scale_ref[...] * v["silu_a"]).astype(jnp.bfloat16)

        def conv_gates():
            v["y_b"] = wide(lambda h: (proj(B_B, h) * v["conv"][h]
                                       * _silu(proj(B_G, h))).astype(jnp.bfloat16))

        def merge_a():
            def tile(h):
                o_a = jnp.dot(v["y_a"], _bf16_rows(w_out_a_ref[:, tile_cols(h)]),
                              preferred_element_type=jnp.float32)
                return jax.nn.sigmoid(proj(M_A, h)) * o_a
            v["merged"] = [tile(h) for h in range(e // COL_TILE)]

        def merge_b():
            def tile(h):
                o_b = jnp.dot(v["y_b"], _bf16_rows(w_out_b_ref[:, tile_cols(h)]),
                              preferred_element_type=jnp.float32)
                return (v["merged"][h] + jax.nn.sigmoid(proj(M_B, h)) * o_b).astype(jnp.bfloat16)
            v["merged"] = wide(tile)

        def output():
            out = wide(lambda h: jnp.dot(v["merged"], _bf16_rows(w_o_ref[:, tile_cols(h)]),
                                         preferred_element_type=jnp.float32))
            x_new = v["x"] + gate * out
            if final_norm:
                x_new = (x_new * _rms_scale(x_new)) * final_g_ref[...]
            o_ref[0, r0:r0 + rc, :] = x_new

        head = [front, pool_value, conv_value]
        body = [pool_gate, pool_mix, conv_gates, merge_a, merge_b]
        return head, body, output

    pending_output = None
    for head, body, output in [make_chunk(c) for c in range(ts // rc)]:
        for stage in head:
            stage()
        if pending_output is not None:
            pending_output()
        for stage in body:
            stage()
        pending_output = output
    pending_output()

    av_ref[0:POOL_HALO, :] = av_ref[ts:ts + POOL_HALO, :]
    u_ref[0:CONV_HALO, :] = u_ref[ts:ts + CONV_HALO, :]


def _resident(shape):
    return pl.BlockSpec(shape, lambda b, s: (0,) * len(shape), pipeline_mode=pl.Buffered(1))


def _block_layer(x, mod, norm_g, b_in, pool_scale, conv_w, conv_b, final_g,
                 w_in, pool_w, w_out_a, w_out_b, w_o, *, final_norm):
    b, seq, d = x.shape
    e = w_out_a.shape[1]
    ts = SEQ_TILE
    assert seq % ts == 0 and ts % ROW_CHUNK == 0 and ROW_CHUNK >= POOL_HALO >= max(POOL_WINDOWS) - 1
    assert CONV_HALO >= CONV_K - 1
    assert w_in.shape == (d, 8 * e) and pool_w.shape[0] == len(POOL_WINDOWS)
    row = lambda v: v.reshape(1, -1)
    small = [row(norm_g), row(b_in), row(pool_scale), conv_w, row(conv_b), row(final_g)]
    weights = [_pack_row_pairs(w) for w in (w_in, pool_w, w_out_a, w_out_b, w_o)]
    return pl.pallas_call(
        functools.partial(_block_kernel, final_norm=final_norm),
        grid=(b, seq // ts),
        in_specs=[pl.BlockSpec((1, ts, d), lambda i, s: (i, s, 0)),
                  pl.BlockSpec((1, 3, d), lambda i, s: (i, 0, 0))]
                 + [_resident(v.shape) for v in small]
                 + [_resident(w.shape) for w in weights],
        out_specs=pl.BlockSpec((1, ts, d), lambda i, s: (i, s, 0)),
        out_shape=jax.ShapeDtypeStruct(x.shape, x.dtype),
        scratch_shapes=[pltpu.VMEM((POOL_HALO + ts, e), jnp.float32),
                        pltpu.VMEM((CONV_HALO + ts, e), jnp.float32)],
        compiler_params=pltpu.CompilerParams(
            dimension_semantics=("arbitrary", "arbitrary"),
            vmem_limit_bytes=VMEM_LIMIT_BYTES),
        name="fused_block",
    )(x, mod, *small, *weights)


def kernel(x, c, ada_w, ada_b, norm_g, w_in, b_in, pool_w, pool_scale, conv_w, conv_b,
           w_out_a, w_out_b, w_o, final_g):
    depth = ada_w.shape[0]
    b, _, d = x.shape
    for l in range(depth):
        mod = _ada_modulation(c, ada_w[l], ada_b[l]).reshape(b, 3, d)
        x = _block_layer(x, mod, norm_g[l], b_in[l], pool_scale[l], conv_w[l], conv_b[l], final_g,
                         w_in[l], pool_w[l], w_out_a[l], w_out_b[l], w_o[l],
                         final_norm=(l == depth - 1))
    return x
```

```python
import functools

import jax
import jax.numpy as jnp
from jax import lax
from jax.experimental import pallas as pl
from jax.experimental.pallas import tpu as pltpu

POOL_WINDOWS = (2, 4, 8, 16)
CONV_K = 3
RMS_EPS = 1e-6

SEQ_TILE = 512
ROW_CHUNK = 256
COL_TILE = 512
POOL_HALO = 16
CONV_HALO = 8
LANES = 128
VMEM_LIMIT_BYTES = 52 * 1024 * 1024
PACK_ROWS, PACK_COLS = 512, 1024

A_V, A_G, B_B, B_C, B_V, B_G, M_A, M_B = range(8)


def _silu(v):
    return v * jax.nn.sigmoid(v)


def _pack_kernel(*refs):
    n = len(refs) // 2
    for w_ref, o_ref in zip(refs[:n], refs[n:]):
        o_ref[...] = pltpu.bitcast(w_ref[...].astype(jnp.bfloat16), jnp.uint32)


def _pack_row_pairs(ws):
    k = ws[0].shape[0]
    assert all(w.shape[0] == k for w in ws) and k % PACK_ROWS == 0
    col_blocks = {max(1, w.shape[1] // PACK_COLS) for w in ws}
    assert len(col_blocks) == 1, "arrays packed together share one column grid"
    nj = col_blocks.pop()
    return pl.pallas_call(
        _pack_kernel,
        grid=(k // PACK_ROWS, nj),
        in_specs=[pl.BlockSpec((PACK_ROWS, w.shape[1] // nj), lambda i, j: (i, j)) for w in ws],
        out_specs=[pl.BlockSpec((PACK_ROWS // 2, w.shape[1] // nj), lambda i, j: (i, j))
                   for w in ws],
        out_shape=[jax.ShapeDtypeStruct((k // 2, w.shape[1]), jnp.uint32) for w in ws],
        compiler_params=pltpu.CompilerParams(dimension_semantics=("arbitrary", "arbitrary")),
        name="pack_weights",
    )(*ws)


def _bf16_rows(packed):
    return pltpu.bitcast(packed, jnp.bfloat16)


def _rms_scale(v):
    return lax.rsqrt(jnp.mean(v * v, axis=-1, keepdims=True) + RMS_EPS)


def _ada_kernel(c_ref, w_ref, b_ref, o_ref):
    c_act = _silu(c_ref[...])
    o_ref[...] = jnp.dot(c_act.astype(jnp.bfloat16), w_ref[...].astype(jnp.bfloat16),
                         preferred_element_type=jnp.float32) + b_ref[...]


def _ada_modulation(c, ada_w, ada_b):
    b, d = c.shape
    n = ada_w.shape[1]
    return pl.pallas_call(
        _ada_kernel,
        grid=(n // d,),
        in_specs=[pl.BlockSpec((b, d), lambda j: (0, 0)),
                  pl.BlockSpec((d, d), lambda j: (0, j)),
                  pl.BlockSpec((1, d), lambda j: (0, j))],
        out_specs=pl.BlockSpec((b, d), lambda j: (0, j)),
        out_shape=jax.ShapeDtypeStruct((b, n), jnp.float32),
        compiler_params=pltpu.CompilerParams(dimension_semantics=("arbitrary",)),
        name="ada_modulation",
    )(c, ada_w, ada_b.reshape(1, n))


def _block_kernel(x_ref, mod_ref, norm_g_ref, b_in_ref, pool_scale_ref, conv_w_ref, conv_b_ref,
                  final_g_ref, w_in_ref, pool_w_ref, w_out_a_ref, w_out_b_ref, w_o_ref,
                  o_ref, av_ref, u_ref, *, final_norm):
    ts = x_ref.shape[1]
    e = w_out_a_ref.shape[1]
    pg = pool_w_ref.shape[2]
    rc = ROW_CHUNK
    s = pl.program_id(1)

    @pl.when(s == 0)
    def _():
        av_ref[0:POOL_HALO, :] = jnp.zeros((POOL_HALO, e), jnp.float32)
        u_ref[0:CONV_HALO, :] = jnp.zeros((CONV_HALO, e), jnp.float32)

    shift = mod_ref[0, 0:1, :]
    scale = mod_ref[0, 1:2, :]
    gate = mod_ref[0, 2:3, :]

    def make_chunk(c):
        r0 = c * rc
        v = {}

        def tile_cols(h, k=0):
            return slice(k * e + h * COL_TILE, k * e + (h + 1) * COL_TILE)

        def proj(k, h):
            cols = tile_cols(h, k)
            return (jnp.dot(v["hb"], _bf16_rows(w_in_ref[:, cols]), preferred_element_type=jnp.float32)
                    + b_in_ref[:, cols])

        def wide(tile_fn):
            return jnp.concatenate([tile_fn(h) for h in range(e // COL_TILE)], axis=1)

        def front():
            x = x_ref[0, r0:r0 + rc, :]
            h = (x * _rms_scale(x)) * norm_g_ref[...] * (1.0 + scale) + shift
            v["x"] = x
            v["hb"] = h.astype(jnp.bfloat16)

        def pool_value():
            for h in range(e // COL_TILE):
                av_ref[POOL_HALO:, tile_cols(h)] = proj(A_V, h)

        def conv_value():
            def tile(h):
                cols = tile_cols(h)
                u = proj(B_C, h) * proj(B_V, h)
                u_ref[CONV_HALO:, cols] = u
                u_ext = u_ref[:, cols]
                u_ref[0:CONV_HALO, cols] = u[rc - CONV_HALO:, :]
                conv = conv_b_ref[:, cols]
                for j in range(CONV_K):
                    lag = CONV_K - 1 - j
                    tap = u if lag == 0 else pltpu.roll(u_ext, lag, 0)[CONV_HALO:, :]
                    conv = conv + tap * conv_w_ref[j:j + 1, cols]
                return conv
            v["conv"] = [tile(h) for h in range(e // COL_TILE)]

        def pool_gate():
            v["silu_a"] = wide(lambda h: _silu(proj(A_G, h)))

        def pool_mix():
            a_ext = av_ref[...]
            av_ref[0:POOL_HALO, :] = a_ext[rc:, :]
            frames_seen = (s * ts + r0 + 1
                           + lax.broadcasted_iota(jnp.int32, (rc, LANES), 0)).astype(jnp.float32)
            mixed = []
            for gi, w in enumerate(POOL_WINDOWS):
                grp = a_ext[:, gi * pg:(gi + 1) * pg]
                wsum = grp
                sh = 1
                while sh < w:
                    wsum = wsum + pltpu.roll(wsum, sh, 0)
                    sh *= 2
                inv_cnt = 1.0 / jnp.minimum(frames_seen, float(w))
                inv_cnt = jnp.concatenate([inv_cnt] * (pg // LANES), axis=1)
                pooled = wsum[POOL_HALO:, :] * inv_cnt - grp[POOL_HALO:, :]
                mixed.append(jnp.dot(pooled.astype(jnp.bfloat16), _bf16_rows(pool_w_ref[gi]),
                                     preferred_element_type=jnp.float32))
            mixed = jnp.concatenate(mixed, axis=1)
            v["y_a"] = (mixed * pool_scale_ref[...] * v["silu_a"]).astype(jnp.bfloat16)

        def conv_gates():
            v["y_b"] = wide(lambda h: (proj(B_B, h) * v["conv"][h]
                                       * _silu(proj(B_G, h))).astype(jnp.bfloat16))

        def merge_a():
            def tile(h):
                o_a = jnp.dot(v["y_a"], _bf16_rows(w_out_a_ref[:, tile_cols(h)]),
                              preferred_element_type=jnp.float32)
                return jax.nn.sigmoid(proj(M_A, h)) * o_a
            v["merged"] = [tile(h) for h in range(e // COL_TILE)]

        def merge_b():
            def tile(h):
                o_b = jnp.dot(v["y_b"], _bf16_rows(w_out_b_ref[:, tile_cols(h)]),
                              preferred_element_type=jnp.float32)
                return (v["merged"][h] + jax.nn.sigmoid(proj(M_B, h)) * o_b).astype(jnp.bfloat16)
            v["merged"] = wide(tile)

        def output():
            out = wide(lambda h: jnp.dot(v["merged"], _bf16_rows(w_o_ref[:, tile_cols(h)]),
                                         preferred_element_type=jnp.float32))
            x_new = v["x"] + gate * out
            if final_norm:
                x_new = (x_new * _rms_scale(x_new)) * final_g_ref[...]
            o_ref[0, r0:r0 + rc, :] = x_new

        head = [front, pool_value, conv_value]
        body = [pool_gate, pool_mix, conv_gates, merge_a, merge_b]
        return head, body, output

    pending_output = None
    for head, body, output in [make_chunk(c) for c in range(ts // rc)]:
        for stage in head:
            stage()
        if pending_output is not None:
            pending_output()
        for stage in body:
            stage()
        pending_output = output
    pending_output()


def _resident(shape):
    return pl.BlockSpec(shape, lambda b, s: (0,) * len(shape), pipeline_mode=pl.Buffered(1))


def _block_layer(x, mod, norm_g, b_in, pool_scale, conv_w, conv_b, final_g,
                 w_in, pool_w, w_out_a, w_out_b, w_o, *, final_norm):
    b, seq, d = x.shape
    e = w_out_a.shape[1]
    ts = SEQ_TILE
    assert seq % ts == 0 and ts % ROW_CHUNK == 0 and ROW_CHUNK >= POOL_HALO >= max(POOL_WINDOWS) - 1
    assert CONV_HALO >= CONV_K - 1
    assert w_in.shape == (d, 8 * e) and pool_w.shape[0] == len(POOL_WINDOWS)
    row = lambda v: v.reshape(1, -1)
    small = [row(norm_g), row(b_in), row(pool_scale), conv_w, row(conv_b), row(final_g)]
    g, pg, _ = pool_w.shape
    (w_in_p,) = _pack_row_pairs([w_in])
    pool_p, w_out_a_p, w_out_b_p, w_o_p = _pack_row_pairs(
        [pool_w.reshape(g * pg, pg), w_out_a, w_out_b, w_o])
    weights = [w_in_p, pool_p.reshape(g, pg // 2, pg), w_out_a_p, w_out_b_p, w_o_p]
    return pl.pallas_call(
        functools.partial(_block_kernel, final_norm=final_norm),
        grid=(b, seq // ts),
        in_specs=[pl.BlockSpec((1, ts, d), lambda i, s: (i, s, 0)),
                  pl.BlockSpec((1, 3, d), lambda i, s: (i, 0, 0))]
                 + [_resident(v.shape) for v in small]
                 + [_resident(w.shape) for w in weights],
        out_specs=pl.BlockSpec((1, ts, d), lambda i, s: (i, s, 0)),
        out_shape=jax.ShapeDtypeStruct(x.shape, x.dtype),
        scratch_shapes=[pltpu.VMEM((POOL_HALO + ROW_CHUNK, e), jnp.float32),
                        pltpu.VMEM((CONV_HALO + ROW_CHUNK, e), jnp.float32)],
        compiler_params=pltpu.CompilerParams(
            dimension_semantics=("arbitrary", "arbitrary"),
            vmem_limit_bytes=VMEM_LIMIT_BYTES),
        name="fused_block",
    )(x, mod, *small, *weights)


def kernel(x, c, ada_w, ada_b, norm_g, w_in, b_in, pool_w, pool_scale, conv_w, conv_b,
           w_out_a, w_out_b, w_o, final_g):
    depth = ada_w.shape[0]
    b, _, d = x.shape
    for l in range(depth):
        mod = _ada_modulation(c, ada_w[l], ada_b[l]).reshape(b, 3, d)
        x = _block_layer(x, mod, norm_g[l], b_in[l], pool_scale[l], conv_w[l], conv_b[l], final_g,
                         w_in[l], pool_w[l], w_out_a[l], w_out_b[l], w_o[l],
                         final_norm=(l == depth - 1))
    return x
```

```python
import functools

import jax
import jax.numpy as jnp
from jax import lax
from jax.experimental import pallas as pl
from jax.experimental.pallas import tpu as pltpu

POOL_WINDOWS = (2, 4, 8, 16)
CONV_K = 3
RMS_EPS = 1e-6

SEQ_TILE = 512
ROW_CHUNK = 256
COL_TILE = 512
POOL_HALO = 16
CONV_HALO = 8
LANES = 128
VMEM_LIMIT_BYTES = 52 * 1024 * 1024
PACK_ROWS, PACK_COLS = 512, 1024

A_V, A_G, B_B, B_C, B_V, B_G, M_A, M_B = range(8)


def _silu(v):
    return v * jax.nn.sigmoid(v)


def _stage_weights(jobs, stage_ref, sem):
    def copy(i):
        src, _, row0, col0, cols = jobs[i]
        return pltpu.make_async_copy(src.at[pl.ds(row0, PACK_ROWS), pl.ds(col0, cols)],
                                     stage_ref.at[i % 2, :, pl.ds(0, cols)], sem.at[i % 2])

    copy(0).start()
    for i, (_, dst, row0, col0, cols) in enumerate(jobs):
        if i + 1 < len(jobs):
            copy(i + 1).start()
        copy(i).wait()
        dst[row0 // 2:(row0 + PACK_ROWS) // 2, col0:col0 + cols] = pltpu.bitcast(
            stage_ref[i % 2, :, 0:cols].astype(jnp.bfloat16), jnp.uint32)


def _bf16_rows(packed):
    return pltpu.bitcast(packed, jnp.bfloat16)


def _rms_scale(v):
    return lax.rsqrt(jnp.mean(v * v, axis=-1, keepdims=True) + RMS_EPS)


def _ada_kernel(c_ref, w_ref, b_ref, o_ref):
    c_act = _silu(c_ref[...])
    o_ref[...] = jnp.dot(c_act.astype(jnp.bfloat16), w_ref[...].astype(jnp.bfloat16),
                         preferred_element_type=jnp.float32) + b_ref[...]


def _ada_modulation(c, ada_w, ada_b):
    b, d = c.shape
    n = ada_w.shape[1]
    return pl.pallas_call(
        _ada_kernel,
        grid=(n // d,),
        in_specs=[pl.BlockSpec((b, d), lambda j: (0, 0)),
                  pl.BlockSpec((d, d), lambda j: (0, j)),
                  pl.BlockSpec((1, d), lambda j: (0, j))],
        out_specs=pl.BlockSpec((b, d), lambda j: (0, j)),
        out_shape=jax.ShapeDtypeStruct((b, n), jnp.float32),
        compiler_params=pltpu.CompilerParams(dimension_semantics=("arbitrary",)),
        name="ada_modulation",
    )(c, ada_w, ada_b.reshape(1, n))


def _block_kernel(x_ref, mod_ref, norm_g_ref, b_in_ref, pool_scale_ref, conv_w_ref, conv_b_ref,
                  final_g_ref, w_in_hbm, pool_w_hbm, w_out_a_hbm, w_out_b_hbm, w_o_hbm,
                  o_ref, av_ref, u_ref, w_in_ref, pool_w_ref, w_out_a_ref, w_out_b_ref, w_o_ref,
                  stage_ref, stage_sem, *, final_norm):
    ts = x_ref.shape[1]
    e = w_out_a_ref.shape[1]
    pg = pool_w_ref.shape[1]
    rc = ROW_CHUNK
    s = pl.program_id(1)

    @pl.when(jnp.logical_and(pl.program_id(0) == 0, s == 0))
    def _():
        jobs = []
        for src, dst in ((w_in_hbm, w_in_ref), (pool_w_hbm, pool_w_ref),
                         (w_out_a_hbm, w_out_a_ref), (w_out_b_hbm, w_out_b_ref),
                         (w_o_hbm, w_o_ref)):
            rows, width = src.shape
            cols = min(width, PACK_COLS)
            jobs += [(src, dst, row0, col0, cols)
                     for col0 in range(0, width, cols) for row0 in range(0, rows, PACK_ROWS)]
        _stage_weights(jobs, stage_ref, stage_sem)

    @pl.when(s == 0)
    def _():
        av_ref[0:POOL_HALO, :] = jnp.zeros((POOL_HALO, e), jnp.float32)
        u_ref[0:CONV_HALO, :] = jnp.zeros((CONV_HALO, e), jnp.float32)

    shift = mod_ref[0, 0:1, :]
    scale = mod_ref[0, 1:2, :]
    gate = mod_ref[0, 2:3, :]

    def make_chunk(c):
        r0 = c * rc
        v = {}

        def tile_cols(h, k=0):
            return slice(k * e + h * COL_TILE, k * e + (h + 1) * COL_TILE)

        def proj(k, h):
            cols = tile_cols(h, k)
            return (jnp.dot(v["hb"], _bf16_rows(w_in_ref[:, cols]), preferred_element_type=jnp.float32)
                    + b_in_ref[:, cols])

        def wide(tile_fn):
            return jnp.concatenate([tile_fn(h) for h in range(e // COL_TILE)], axis=1)

        def front():
            x = x_ref[0, r0:r0 + rc, :]
            h = (x * _rms_scale(x)) * norm_g_ref[...] * (1.0 + scale) + shift
            v["x"] = x
            v["hb"] = h.astype(jnp.bfloat16)

        def pool_value():
            for h in range(e // COL_TILE):
                av_ref[POOL_HALO:, tile_cols(h)] = proj(A_V, h)

        def conv_value():
            def tile(h):
                cols = tile_cols(h)
                u = proj(B_C, h) * proj(B_V, h)
                u_ref[CONV_HALO:, cols] = u
                u_ext = u_ref[:, cols]
                u_ref[0:CONV_HALO, cols] = u[rc - CONV_HALO:, :]
                conv = conv_b_ref[:, cols]
                for j in range(CONV_K):
                    lag = CONV_K - 1 - j
                    tap = u if lag == 0 else pltpu.roll(u_ext, lag, 0)[CONV_HALO:, :]
                    conv = conv + tap * conv_w_ref[j:j + 1, cols]
                return conv
            v["conv"] = [tile(h) for h in range(e // COL_TILE)]

        def pool_gate():
            v["silu_a"] = wide(lambda h: _silu(proj(A_G, h)))

        def pool_mix():
            a_ext = av_ref[...]
            av_ref[0:POOL_HALO, :] = a_ext[rc:, :]
            frames_seen = (s * ts + r0 + 1
                           + lax.broadcasted_iota(jnp.int32, (rc, LANES), 0)).astype(jnp.float32)
            mixed = []
            for gi, w in enumerate(POOL_WINDOWS):
                grp = a_ext[:, gi * pg:(gi + 1) * pg]
                wsum = grp
                sh = 1
                while sh < w:
                    wsum = wsum + pltpu.roll(wsum, sh, 0)
                    sh *= 2
                inv_cnt = 1.0 / jnp.minimum(frames_seen, float(w))
                inv_cnt = jnp.concatenate([inv_cnt] * (pg // LANES), axis=1)
                pooled = wsum[POOL_HALO:, :] * inv_cnt - grp[POOL_HALO:, :]
                group_w = _bf16_rows(pool_w_ref[gi * pg // 2:(gi + 1) * pg // 2, :])
                mixed.append(jnp.dot(pooled.astype(jnp.bfloat16), group_w,
                                     preferred_element_type=jnp.float32))
            mixed = jnp.concatenate(mixed, axis=1)
            v["y_a"] = (mixed * pool_scale_ref[...] * v["silu_a"]).astype(jnp.bfloat16)

        def conv_gates():
            v["y_b"] = wide(lambda h: (proj(B_B, h) * v["conv"][h]
                                       * _silu(proj(B_G, h))).astype(jnp.bfloat16))

        def merge_a():
            def tile(h):
                o_a = jnp.dot(v["y_a"], _bf16_rows(w_out_a_ref[:, tile_cols(h)]),
                              preferred_element_type=jnp.float32)
                return jax.nn.sigmoid(proj(M_A, h)) * o_a
            v["merged"] = [tile(h) for h in range(e // COL_TILE)]

        def merge_b():
            def tile(h):
                o_b = jnp.dot(v["y_b"], _bf16_rows(w_out_b_ref[:, tile_cols(h)]),
                              preferred_element_type=jnp.float32)
                return (v["merged"][h] + jax.nn.sigmoid(proj(M_B, h)) * o_b).astype(jnp.bfloat16)
            v["merged"] = wide(tile)

        def output():
            out = wide(lambda h: jnp.dot(v["merged"], _bf16_rows(w_o_ref[:, tile_cols(h)]),
                                         preferred_element_type=jnp.float32))
            x_new = v["x"] + gate * out
            if final_norm:
                x_new = (x_new * _rms_scale(x_new)) * final_g_ref[...]
            o_ref[0, r0:r0 + rc, :] = x_new

        head = [front, pool_value, conv_value]
        body = [pool_gate, pool_mix, conv_gates, merge_a, merge_b]
        return head, body, output

    pending_output = None
    for head, body, output in [make_chunk(c) for c in range(ts // rc)]:
        for stage in head:
            stage()
        if pending_output is not None:
            pending_output()
        for stage in body:
            stage()
        pending_output = output
    pending_output()


def _resident(shape):
    return pl.BlockSpec(shape, lambda b, s: (0,) * len(shape), pipeline_mode=pl.Buffered(1))


def _block_layer(x, mod, norm_g, b_in, pool_scale, conv_w, conv_b, final_g,
                 w_in, pool_w, w_out_a, w_out_b, w_o, *, final_norm):
    b, seq, d = x.shape
    e = w_out_a.shape[1]
    ts = SEQ_TILE
    assert seq % ts == 0 and ts % ROW_CHUNK == 0 and ROW_CHUNK >= POOL_HALO >= max(POOL_WINDOWS) - 1
    assert CONV_HALO >= CONV_K - 1
    assert w_in.shape == (d, 8 * e) and pool_w.shape[0] == len(POOL_WINDOWS)
    row = lambda v: v.reshape(1, -1)
    small = [row(norm_g), row(b_in), row(pool_scale), conv_w, row(conv_b), row(final_g)]
    g, pg, _ = pool_w.shape
    weights = [w_in, pool_w.reshape(g * pg, pg), w_out_a, w_out_b, w_o]
    assert all(w.shape[0] % PACK_ROWS == 0 and w.shape[1] % min(w.shape[1], PACK_COLS) == 0
               for w in weights)
    packed = [pltpu.VMEM((w.shape[0] // 2, w.shape[1]), jnp.uint32) for w in weights]
    return pl.pallas_call(
        functools.partial(_block_kernel, final_norm=final_norm),
        grid=(b, seq // ts),
        in_specs=[pl.BlockSpec((1, ts, d), lambda i, s: (i, s, 0)),
                  pl.BlockSpec((1, 3, d), lambda i, s: (i, 0, 0))]
                 + [_resident(v.shape) for v in small]
                 + [pl.BlockSpec(memory_space=pl.ANY) for _ in weights],
        out_specs=pl.BlockSpec((1, ts, d), lambda i, s: (i, s, 0)),
        out_shape=jax.ShapeDtypeStruct(x.shape, x.dtype),
        scratch_shapes=[pltpu.VMEM((POOL_HALO + ROW_CHUNK, e), jnp.float32),
                        pltpu.VMEM((CONV_HALO + ROW_CHUNK, e), jnp.float32)]
                       + packed
                       + [pltpu.VMEM((2, PACK_ROWS, PACK_COLS), jnp.float32),
                          pltpu.SemaphoreType.DMA((2,))],
        compiler_params=pltpu.CompilerParams(
            dimension_semantics=("arbitrary", "arbitrary"),
            vmem_limit_bytes=VMEM_LIMIT_BYTES),
        name="fused_block",
    )(x, mod, *small, *weights)


def kernel(x, c, ada_w, ada_b, norm_g, w_in, b_in, pool_w, pool_scale, conv_w, conv_b,
           w_out_a, w_out_b, w_o, final_g):
    depth = ada_w.shape[0]
    b, _, d = x.shape
    for l in range(depth):
        mod = _ada_modulation(c, ada_w[l], ada_b[l]).reshape(b, 3, d)
        x = _block_layer(x, mod, norm_g[l], b_in[l], pool_scale[l], conv_w[l], conv_b[l], final_g,
                         w_in[l], pool_w[l], w_out_a[l], w_out_b[l], w_o[l],
                         final_norm=(l == depth - 1))
    return x
```

```python
import functools

import jax
import jax.numpy as jnp
from jax import lax
from jax.experimental import pallas as pl
from jax.experimental.pallas import tpu as pltpu

POOL_WINDOWS = (2, 4, 8, 16)
CONV_K = 3
RMS_EPS = 1e-6

SEQ_TILE = 512
ROW_CHUNK = 256
COL_TILE = 256
POOL_HALO = 16
CONV_HALO = 8
LANES = 128
VMEM_LIMIT_BYTES = 52 * 1024 * 1024
PACK_ROWS, PACK_COLS = 512, 1024
STAGE_SLOTS = 4

A_V, A_G, B_B, B_C, B_V, B_G, M_A, M_B = range(8)


def _silu(v):
    return v * jax.nn.sigmoid(v)


def _stage_weights(jobs, stage_ref, sem):
    n_slots = stage_ref.shape[0]

    def copy(i):
        src, _, row0, col0, cols = jobs[i]
        slot = i % n_slots
        return pltpu.make_async_copy(src.at[pl.ds(row0, PACK_ROWS), pl.ds(col0, cols)],
                                     stage_ref.at[slot, :, pl.ds(0, cols)], sem.at[slot])

    for i in range(min(n_slots - 1, len(jobs))):
        copy(i).start()
    for i, (_, dst, row0, col0, cols) in enumerate(jobs):
        if i + n_slots - 1 < len(jobs):
            copy(i + n_slots - 1).start()
        copy(i).wait()
        dst[row0 // 2:(row0 + PACK_ROWS) // 2, col0:col0 + cols] = pltpu.bitcast(
            stage_ref[i % n_slots, :, 0:cols].astype(jnp.bfloat16), jnp.uint32)


def _bf16_rows(packed):
    return pltpu.bitcast(packed, jnp.bfloat16)


def _rms_scale(v):
    return lax.rsqrt(jnp.mean(v * v, axis=-1, keepdims=True) + RMS_EPS)


def _ada_kernel(c_ref, w_ref, b_ref, o_ref):
    c_act = _silu(c_ref[...])
    o_ref[...] = jnp.dot(c_act.astype(jnp.bfloat16), w_ref[...].astype(jnp.bfloat16),
                         preferred_element_type=jnp.float32) + b_ref[...]


def _ada_modulation(c, ada_w, ada_b):
    b, d = c.shape
    n = ada_w.shape[1]
    return pl.pallas_call(
        _ada_kernel,
        grid=(n // d,),
        in_specs=[pl.BlockSpec((b, d), lambda j: (0, 0)),
                  pl.BlockSpec((d, d), lambda j: (0, j)),
                  pl.BlockSpec((1, d), lambda j: (0, j))],
        out_specs=pl.BlockSpec((b, d), lambda j: (0, j)),
        out_shape=jax.ShapeDtypeStruct((b, n), jnp.float32),
        compiler_params=pltpu.CompilerParams(dimension_semantics=("arbitrary",)),
        name="ada_modulation",
    )(c, ada_w, ada_b.reshape(1, n))


def _block_kernel(x_ref, mod_ref, norm_g_ref, b_in_ref, pool_scale_ref, conv_w_ref, conv_b_ref,
                  final_g_ref, w_in_hbm, pool_w_hbm, w_out_a_hbm, w_out_b_hbm, w_o_hbm,
                  o_ref, av_ref, u_ref, w_in_ref, pool_w_ref, w_out_a_ref, w_out_b_ref, w_o_ref,
                  stage_ref, stage_sem, *, final_norm):
    ts = x_ref.shape[1]
    e = w_out_a_ref.shape[1]
    pg = pool_w_ref.shape[1]
    rc = ROW_CHUNK
    s = pl.program_id(1)

    @pl.when(jnp.logical_and(pl.program_id(0) == 0, s == 0))
    def _():
        jobs = []
        for src, dst in ((w_in_hbm, w_in_ref), (pool_w_hbm, pool_w_ref),
                         (w_out_a_hbm, w_out_a_ref), (w_out_b_hbm, w_out_b_ref),
                         (w_o_hbm, w_o_ref)):
            rows, width = src.shape
            cols = min(width, PACK_COLS)
            jobs += [(src, dst, row0, col0, cols)
                     for col0 in range(0, width, cols) for row0 in range(0, rows, PACK_ROWS)]
        _stage_weights(jobs, stage_ref, stage_sem)

    @pl.when(s == 0)
    def _():
        av_ref[0:POOL_HALO, :] = jnp.zeros((POOL_HALO, e), jnp.float32)
        u_ref[0:CONV_HALO, :] = jnp.zeros((CONV_HALO, e), jnp.float32)

    shift = mod_ref[0, 0:1, :]
    scale = mod_ref[0, 1:2, :]
    gate = mod_ref[0, 2:3, :]

    def make_chunk(c):
        r0 = c * rc
        v = {}

        def tile_cols(h, k=0):
            return slice(k * e + h * COL_TILE, k * e + (h + 1) * COL_TILE)

        def proj(k, h):
            cols = tile_cols(h, k)
            return (jnp.dot(v["hb"], _bf16_rows(w_in_ref[:, cols]), preferred_element_type=jnp.float32)
                    + b_in_ref[:, cols])

        def wide(tile_fn):
            return jnp.concatenate([tile_fn(h) for h in range(e // COL_TILE)], axis=1)

        def front():
            x = x_ref[0, r0:r0 + rc, :]
            h = (x * _rms_scale(x)) * norm_g_ref[...] * (1.0 + scale) + shift
            v["x"] = x
            v["hb"] = h.astype(jnp.bfloat16)

        def pool_value():
            for h in range(e // COL_TILE):
                av_ref[POOL_HALO:, tile_cols(h)] = proj(A_V, h)

        def conv_value():
            def tile(h):
                cols = tile_cols(h)
                u = proj(B_C, h) * proj(B_V, h)
                u_ref[CONV_HALO:, cols] = u
                u_ext = u_ref[:, cols]
                u_ref[0:CONV_HALO, cols] = u[rc - CONV_HALO:, :]
                conv = conv_b_ref[:, cols]
                for j in range(CONV_K):
                    lag = CONV_K - 1 - j
                    tap = u if lag == 0 else pltpu.roll(u_ext, lag, 0)[CONV_HALO:, :]
                    conv = conv + tap * conv_w_ref[j:j + 1, cols]
                return conv
            v["conv"] = [tile(h) for h in range(e // COL_TILE)]

        def pool_gate():
            v["silu_a"] = wide(lambda h: _silu(proj(A_G, h)))

        def pool_mix():
            a_ext = av_ref[...]
            av_ref[0:POOL_HALO, :] = a_ext[rc:, :]
            frames_seen = (s * ts + r0 + 1
                           + lax.broadcasted_iota(jnp.int32, (rc, LANES), 0)).astype(jnp.float32)
            mixed = []
            for gi, w in enumerate(POOL_WINDOWS):
                grp = a_ext[:, gi * pg:(gi + 1) * pg]
                wsum = grp
                sh = 1
                while sh < w:
                    wsum = wsum + pltpu.roll(wsum, sh, 0)
                    sh *= 2
                inv_cnt = 1.0 / jnp.minimum(frames_seen, float(w))
                inv_cnt = jnp.concatenate([inv_cnt] * (pg // LANES), axis=1)
                pooled = wsum[POOL_HALO:, :] * inv_cnt - grp[POOL_HALO:, :]
                group_w = _bf16_rows(pool_w_ref[gi * pg // 2:(gi + 1) * pg // 2, :])
                mixed.append(jnp.dot(pooled.astype(jnp.bfloat16), group_w,
                                     preferred_element_type=jnp.float32))
            mixed = jnp.concatenate(mixed, axis=1)
            v["y_a"] = (mixed * pool_scale_ref[...] * v["silu_a"]).astype(jnp.bfloat16)

        def conv_gates():
            v["y_b"] = wide(lambda h: (proj(B_B, h) * v["conv"][h]
                                       * _silu(proj(B_G, h))).astype(jnp.bfloat16))

        def merge_a():
            def tile(h):
                o_a = jnp.dot(v["y_a"], _bf16_rows(w_out_a_ref[:, tile_cols(h)]),
                              preferred_element_type=jnp.float32)
                return jax.nn.sigmoid(proj(M_A, h)) * o_a
            v["merged"] = [tile(h) for h in range(e // COL_TILE)]

        def merge_b():
            def tile(h):
                o_b = jnp.dot(v["y_b"], _bf16_rows(w_out_b_ref[:, tile_cols(h)]),
                              preferred_element_type=jnp.float32)
                return (v["merged"][h] + jax.nn.sigmoid(proj(M_B, h)) * o_b).astype(jnp.bfloat16)
            v["merged"] = wide(tile)

        def output():
            out = wide(lambda h: jnp.dot(v["merged"], _bf16_rows(w_o_ref[:, tile_cols(h)]),
                                         preferred_element_type=jnp.float32))
            x_new = v["x"] + gate * out
            if final_norm:
                x_new = (x_new * _rms_scale(x_new)) * final_g_ref[...]
            o_ref[0, r0:r0 + rc, :] = x_new

        head = [front, pool_value, conv_value]
        body = [pool_gate, pool_mix, conv_gates, merge_a, merge_b]
        return head, body, output

    pending_output = None
    for head, body, output in [make_chunk(c) for c in range(ts // rc)]:
        for stage in head:
            stage()
        if pending_output is not None:
            pending_output()
        for stage in body:
            stage()
        pending_output = output
    pending_output()


def _resident(shape):
    return pl.BlockSpec(shape, lambda b, s: (0,) * len(shape), pipeline_mode=pl.Buffered(1))


def _block_layer(x, mod, norm_g, b_in, pool_scale, conv_w, conv_b, final_g,
                 w_in, pool_w, w_out_a, w_out_b, w_o, *, final_norm):
    b, seq, d = x.shape
    e = w_out_a.shape[1]
    ts = SEQ_TILE
    assert seq % ts == 0 and ts % ROW_CHUNK == 0 and ROW_CHUNK >= POOL_HALO >= max(POOL_WINDOWS) - 1
    assert CONV_HALO >= CONV_K - 1
    assert w_in.shape == (d, 8 * e) and pool_w.shape[0] == len(POOL_WINDOWS)
    row = lambda v: v.reshape(1, -1)
    small = [row(norm_g), row(b_in), row(pool_scale), conv_w, row(conv_b), row(final_g)]
    g, pg, _ = pool_w.shape
    weights = [w_in, pool_w.reshape(g * pg, pg), w_out_a, w_out_b, w_o]
    assert all(w.shape[0] % PACK_ROWS == 0 and w.shape[1] % min(w.shape[1], PACK_COLS) == 0
               for w in weights)
    packed = [pltpu.VMEM((w.shape[0] // 2, w.shape[1]), jnp.uint32) for w in weights]
    return pl.pallas_call(
        functools.partial(_block_kernel, final_norm=final_norm),
        grid=(b, seq // ts),
        in_specs=[pl.BlockSpec((1, ts, d), lambda i, s: (i, s, 0)),
                  pl.BlockSpec((1, 3, d), lambda i, s: (i, 0, 0))]
                 + [_resident(v.shape) for v in small]
                 + [pl.BlockSpec(memory_space=pl.ANY) for _ in weights],
        out_specs=pl.BlockSpec((1, ts, d), lambda i, s: (i, s, 0)),
        out_shape=jax.ShapeDtypeStruct(x.shape, x.dtype),
        scratch_shapes=[pltpu.VMEM((POOL_HALO + ROW_CHUNK, e), jnp.float32),
                        pltpu.VMEM((CONV_HALO + ROW_CHUNK, e), jnp.float32)]
                       + packed
                       + [pltpu.VMEM((STAGE_SLOTS, PACK_ROWS, PACK_COLS), jnp.float32),
                          pltpu.SemaphoreType.DMA((STAGE_SLOTS,))],
        compiler_params=pltpu.CompilerParams(
            dimension_semantics=("arbitrary", "arbitrary"),
            vmem_limit_bytes=VMEM_LIMIT_BYTES),
        name="fused_block",
    )(x, mod, *small, *weights)


def kernel(x, c, ada_w, ada_b, norm_g, w_in, b_in, pool_w, pool_scale, conv_w, conv_b,
           w_out_a, w_out_b, w_o, final_g):
    depth = ada_w.shape[0]
    b, _, d = x.shape
    for l in range(depth):
        mod = _ada_modulation(c, ada_w[l], ada_b[l]).reshape(b, 3, d)
        x = _block_layer(x, mod, norm_g[l], b_in[l], pool_scale[l], conv_w[l], conv_b[l], final_g,
                         w_in[l], pool_w[l], w_out_a[l], w_out_b[l], w_o[l],
                         final_norm=(l == depth - 1))
    return x
```

```python
import functools

import jax
import jax.numpy as jnp
from jax import lax
from jax.experimental import pallas as pl
from jax.experimental.pallas import tpu as pltpu

POOL_WINDOWS = (2, 4, 8, 16)
CONV_K = 3
RMS_EPS = 1e-6

SEQ_TILE = 512
ROW_CHUNK = 256
COL_TILE = 256
POOL_HALO = 16
CONV_HALO = 8
LANES = 128
VMEM_LIMIT_BYTES = 52 * 1024 * 1024
PACK_ROWS, PACK_COLS = 512, 1024
STAGE_SLOTS = 4

A_V, A_G, B_B, B_C, B_V, B_G, M_A, M_B = range(8)


def _silu(v):
    return v * jax.nn.sigmoid(v)


def _stage_weights(jobs, stage_ref, sem):
    n_slots = stage_ref.shape[0]

    def copy(i):
        src, _, row0, col0, cols = jobs[i]
        slot = i % n_slots
        return pltpu.make_async_copy(src.at[pl.ds(row0, PACK_ROWS), pl.ds(col0, cols)],
                                     stage_ref.at[slot, :, pl.ds(0, cols)], sem.at[slot])

    for i in range(min(n_slots - 1, len(jobs))):
        copy(i).start()
    for i, (_, dst, row0, col0, cols) in enumerate(jobs):
        if i + n_slots - 1 < len(jobs):
            copy(i + n_slots - 1).start()
        copy(i).wait()
        dst[row0 // 2:(row0 + PACK_ROWS) // 2, col0:col0 + cols] = pltpu.bitcast(
            stage_ref[i % n_slots, :, 0:cols].astype(jnp.bfloat16), jnp.uint32)


def _bf16_rows(packed):
    return pltpu.bitcast(packed, jnp.bfloat16)


def _rms_scale(v):
    return lax.rsqrt(jnp.mean(v * v, axis=-1, keepdims=True) + RMS_EPS)


def _ada_kernel(c_ref, w_ref, b_ref, o_ref):
    c_act = _silu(c_ref[...])
    o_ref[...] = jnp.dot(c_act.astype(jnp.bfloat16), w_ref[...].astype(jnp.bfloat16),
                         preferred_element_type=jnp.float32) + b_ref[...]


def _ada_modulation(c, ada_w, ada_b):
    b, d = c.shape
    n = ada_w.shape[1]
    return pl.pallas_call(
        _ada_kernel,
        grid=(n // d,),
        in_specs=[pl.BlockSpec((b, d), lambda j: (0, 0)),
                  pl.BlockSpec((d, d), lambda j: (0, j)),
                  pl.BlockSpec((1, d), lambda j: (0, j))],
        out_specs=pl.BlockSpec((b, d), lambda j: (0, j)),
        out_shape=jax.ShapeDtypeStruct((b, n), jnp.float32),
        compiler_params=pltpu.CompilerParams(dimension_semantics=("arbitrary",)),
        name="ada_modulation",
    )(c, ada_w, ada_b.reshape(1, n))


def _block_kernel(x_ref, mod_ref, norm_g_ref, b_in_ref, pool_scale_ref, conv_w_ref, conv_b_ref,
                  final_g_ref, w_in_hbm, pool_w_hbm, w_out_a_hbm, w_out_b_hbm, w_o_hbm,
                  o_ref, av_ref, u_ref, w_in_ref, pool_w_ref, w_out_a_ref, w_out_b_ref, w_o_ref,
                  stage_ref, stage_sem, *, final_norm):
    ts = x_ref.shape[1]
    e = w_out_a_ref.shape[1]
    pg = pool_w_ref.shape[1]
    rc = ROW_CHUNK
    s = pl.program_id(1)

    @pl.when(jnp.logical_and(pl.program_id(0) == 0, s == 0))
    def _():
        jobs = []
        for src, dst in ((w_in_hbm, w_in_ref), (pool_w_hbm, pool_w_ref),
                         (w_out_a_hbm, w_out_a_ref), (w_out_b_hbm, w_out_b_ref),
                         (w_o_hbm, w_o_ref)):
            rows, width = src.shape
            cols = min(width, PACK_COLS)
            jobs += [(src, dst, row0, col0, cols)
                     for col0 in range(0, width, cols) for row0 in range(0, rows, PACK_ROWS)]
        _stage_weights(jobs, stage_ref, stage_sem)

    @pl.when(s == 0)
    def _():
        av_ref[0:POOL_HALO, :] = jnp.zeros((POOL_HALO, e), jnp.float32)
        u_ref[0:CONV_HALO, :] = jnp.zeros((CONV_HALO, e), jnp.float32)

    shift = mod_ref[0, 0:1, :]
    gain = norm_g_ref[...] * (1.0 + mod_ref[0, 1:2, :])
    gate = mod_ref[0, 2:3, :]

    def make_chunk(c):
        r0 = c * rc
        v = {}

        def tile_cols(h, k=0):
            return slice(k * e + h * COL_TILE, k * e + (h + 1) * COL_TILE)

        def proj(k, h):
            cols = tile_cols(h, k)
            return (jnp.dot(v["hb"], _bf16_rows(w_in_ref[:, cols]), preferred_element_type=jnp.float32)
                    + b_in_ref[:, cols])

        def wide(tile_fn):
            return jnp.concatenate([tile_fn(h) for h in range(e // COL_TILE)], axis=1)

        def front():
            x = x_ref[0, r0:r0 + rc, :]
            h = (x * _rms_scale(x)) * gain + shift
            v["hb"] = h.astype(jnp.bfloat16)

        def pool_value():
            for h in range(e // COL_TILE):
                av_ref[POOL_HALO:, tile_cols(h)] = proj(A_V, h)

        def conv_value():
            def tile(h):
                cols = tile_cols(h)
                u = proj(B_C, h) * proj(B_V, h)
                u_ref[CONV_HALO:, cols] = u
                u_ext = u_ref[:, cols]
                u_ref[0:CONV_HALO, cols] = u[rc - CONV_HALO:, :]
                conv = conv_b_ref[:, cols]
                for j in range(CONV_K):
                    lag = CONV_K - 1 - j
                    tap = u if lag == 0 else pltpu.roll(u_ext, lag, 0)[CONV_HALO:, :]
                    conv = conv + tap * conv_w_ref[j:j + 1, cols]
                return conv
            v["conv"] = [tile(h) for h in range(e // COL_TILE)]

        def pool_gate():
            v["silu_a"] = wide(lambda h: _silu(proj(A_G, h)))

        def pool_mix():
            a_ext = av_ref[...]
            av_ref[0:POOL_HALO, :] = a_ext[rc:, :]
            frames_seen = (s * ts + r0 + 1
                           + lax.broadcasted_iota(jnp.int32, (rc, LANES), 0)).astype(jnp.float32)
            mixed = []
            for gi, w in enumerate(POOL_WINDOWS):
                grp = a_ext[:, gi * pg:(gi + 1) * pg]
                wsum = grp
                sh = 1
                while sh < w:
                    wsum = wsum + pltpu.roll(wsum, sh, 0)
                    sh *= 2
                inv_cnt = 1.0 / jnp.minimum(frames_seen, float(w))
                inv_cnt = jnp.concatenate([inv_cnt] * (pg // LANES), axis=1)
                pooled = wsum[POOL_HALO:, :] * inv_cnt - grp[POOL_HALO:, :]
                group_w = _bf16_rows(pool_w_ref[gi * pg // 2:(gi + 1) * pg // 2, :])
                mixed.append(jnp.dot(pooled.astype(jnp.bfloat16), group_w,
                                     preferred_element_type=jnp.float32))
            mixed = jnp.concatenate(mixed, axis=1)
            v["y_a"] = (mixed * pool_scale_ref[...] * v["silu_a"]).astype(jnp.bfloat16)

        def conv_gates():
            v["y_b"] = wide(lambda h: (proj(B_B, h) * v["conv"][h]
                                       * _silu(proj(B_G, h))).astype(jnp.bfloat16))

        def merge_a():
            def tile(h):
                o_a = jnp.dot(v["y_a"], _bf16_rows(w_out_a_ref[:, tile_cols(h)]),
                              preferred_element_type=jnp.float32)
                return jax.nn.sigmoid(proj(M_A, h)) * o_a
            v["merged"] = [tile(h) for h in range(e // COL_TILE)]

        def merge_b():
            def tile(h):
                o_b = jnp.dot(v["y_b"], _bf16_rows(w_out_b_ref[:, tile_cols(h)]),
                              preferred_element_type=jnp.float32)
                return (v["merged"][h] + jax.nn.sigmoid(proj(M_B, h)) * o_b).astype(jnp.bfloat16)
            v["merged"] = wide(tile)

        def output():
            out = wide(lambda h: jnp.dot(v["merged"], _bf16_rows(w_o_ref[:, tile_cols(h)]),
                                         preferred_element_type=jnp.float32))
            x_new = x_ref[0, r0:r0 + rc, :] + gate * out
            if final_norm:
                x_new = (x_new * _rms_scale(x_new)) * final_g_ref[...]
            o_ref[0, r0:r0 + rc, :] = x_new

        head = [front, pool_value, conv_value]
        body = [pool_gate, pool_mix, conv_gates, merge_a, merge_b]
        return head, body, output

    pending_output = None
    for head, body, output in [make_chunk(c) for c in range(ts // rc)]:
        for stage in head:
            stage()
        if pending_output is not None:
            pending_output()
        for stage in body:
            stage()
        pending_output = output
    pending_output()


def _resident(shape):
    return pl.BlockSpec(shape, lambda b, s: (0,) * len(shape), pipeline_mode=pl.Buffered(1))


def _block_layer(x, mod, norm_g, b_in, pool_scale, conv_w, conv_b, final_g,
                 w_in, pool_w, w_out_a, w_out_b, w_o, *, final_norm):
    b, seq, d = x.shape
    e = w_out_a.shape[1]
    ts = SEQ_TILE
    assert seq % ts == 0 and ts % ROW_CHUNK == 0 and ROW_CHUNK >= POOL_HALO >= max(POOL_WINDOWS) - 1
    assert CONV_HALO >= CONV_K - 1
    assert w_in.shape == (d, 8 * e) and pool_w.shape[0] == len(POOL_WINDOWS)
    row = lambda v: v.reshape(1, -1)
    small = [row(norm_g), row(b_in), row(pool_scale), conv_w, row(conv_b), row(final_g)]
    g, pg, _ = pool_w.shape
    weights = [w_in, pool_w.reshape(g * pg, pg), w_out_a, w_out_b, w_o]
    assert all(w.shape[0] % PACK_ROWS == 0 and w.shape[1] % min(w.shape[1], PACK_COLS) == 0
               for w in weights)
    packed = [pltpu.VMEM((w.shape[0] // 2, w.shape[1]), jnp.uint32) for w in weights]
    return pl.pallas_call(
        functools.partial(_block_kernel, final_norm=final_norm),
        grid=(b, seq // ts),
        in_specs=[pl.BlockSpec((1, ts, d), lambda i, s: (i, s, 0)),
                  pl.BlockSpec((1, 3, d), lambda i, s: (i, 0, 0))]
                 + [_resident(v.shape) for v in small]
                 + [pl.BlockSpec(memory_space=pl.ANY) for _ in weights],
        out_specs=pl.BlockSpec((1, ts, d), lambda i, s: (i, s, 0)),
        out_shape=jax.ShapeDtypeStruct(x.shape, x.dtype),
        scratch_shapes=[pltpu.VMEM((POOL_HALO + ROW_CHUNK, e), jnp.float32),
                        pltpu.VMEM((CONV_HALO + ROW_CHUNK, e), jnp.float32)]
                       + packed
                       + [pltpu.VMEM((STAGE_SLOTS, PACK_ROWS, PACK_COLS), jnp.float32),
                          pltpu.SemaphoreType.DMA((STAGE_SLOTS,))],
        compiler_params=pltpu.CompilerParams(
            dimension_semantics=("arbitrary", "arbitrary"),
            vmem_limit_bytes=VMEM_LIMIT_BYTES),
        name="fused_block",
    )(x, mod, *small, *weights)


def kernel(x, c, ada_w, ada_b, norm_g, w_in, b_in, pool_w, pool_scale, conv_w, conv_b,
           w_out_a, w_out_b, w_o, final_g):
    depth = ada_w.shape[0]
    b, _, d = x.shape
    for l in range(depth):
        mod = _ada_modulation(c, ada_w[l], ada_b[l]).reshape(b, 3, d)
        x = _block_layer(x, mod, norm_g[l], b_in[l], pool_scale[l], conv_w[l], conv_b[l], final_g,
                         w_in[l], pool_w[l], w_out_a[l], w_out_b[l], w_o[l],
                         final_norm=(l == depth - 1))
    return x
```

```python
import functools

import jax
import jax.numpy as jnp
from jax import lax
from jax.experimental import pallas as pl
from jax.experimental.pallas import tpu as pltpu

POOL_WINDOWS = (2, 4, 8, 16)
CONV_K = 3
RMS_EPS = 1e-6

SEQ_TILE = 512
ROW_CHUNK = 256
COL_TILE = 256
POOL_HALO = 16
CONV_HALO = 8
LANES = 128
VMEM_LIMIT_BYTES = 52 * 1024 * 1024
PACK_ROWS, PACK_COLS = 512, 1024
STAGE_SLOTS = 4

A_V, A_G, B_B, B_C, B_V, B_G, M_A, M_B = range(8)


def _silu(v):
    return v * jax.nn.sigmoid(v)


def _block_jobs(src, sink):
    rows, width = src.shape
    cols = min(width, PACK_COLS)
    return [(src, row0, col0, cols, sink)
            for col0 in range(0, width, cols) for row0 in range(0, rows, PACK_ROWS)]


def _stream_blocks(jobs, stage_ref, sem):
    n_slots = stage_ref.shape[0]

    def copy(i):
        src, row0, col0, cols, _ = jobs[i]
        slot = i % n_slots
        return pltpu.make_async_copy(src.at[pl.ds(row0, PACK_ROWS), pl.ds(col0, cols)],
                                     stage_ref.at[slot, :, pl.ds(0, cols)], sem.at[slot])

    for i in range(min(n_slots - 1, len(jobs))):
        copy(i).start()
    for i, (_, row0, col0, cols, sink) in enumerate(jobs):
        if i + n_slots - 1 < len(jobs):
            copy(i + n_slots - 1).start()
        copy(i).wait()
        sink(row0, col0, stage_ref[i % n_slots, :, 0:cols].astype(jnp.bfloat16))


def _pack_into(dst):
    def sink(row0, col0, block):
        rows, cols = block.shape
        dst[row0 // 2:(row0 + rows) // 2, col0:col0 + cols] = pltpu.bitcast(block, jnp.uint32)
    return sink


def _bf16_rows(packed):
    return pltpu.bitcast(packed, jnp.bfloat16)


def _rms_scale(v):
    return lax.rsqrt(jnp.mean(v * v, axis=-1, keepdims=True) + RMS_EPS)


def _block_kernel(x_ref, c_ref, ada_b_ref, norm_g_ref, b_in_ref, pool_scale_ref, conv_w_ref,
                  conv_b_ref, final_g_ref,
                  ada_w_hbm, w_in_hbm, pool_w_hbm, w_out_a_hbm, w_out_b_hbm, w_o_hbm,
                  o_ref, av_ref, u_ref, mod_ref,
                  w_in_ref, pool_w_ref, w_out_a_ref, w_out_b_ref, w_o_ref,
                  stage_ref, stage_sem, *, final_norm):
    ts = x_ref.shape[1]
    d = x_ref.shape[2]
    e = w_out_a_ref.shape[1]
    pg = pool_w_ref.shape[1]
    rc = ROW_CHUNK
    s = pl.program_id(1)

    @pl.when(jnp.logical_and(pl.program_id(0) == 0, s == 0))
    def _():
        c_act = _silu(c_ref[...]).astype(jnp.bfloat16)
        mod_ref[...] = jnp.broadcast_to(ada_b_ref[...], mod_ref.shape)

        def add_modulation(row0, col0, block):
            rows, cols = block.shape
            mod_ref[:, col0:col0 + cols] += jnp.dot(c_act[:, row0:row0 + rows], block,
                                                    preferred_element_type=jnp.float32)

        jobs = _block_jobs(ada_w_hbm, add_modulation)
        for src, dst in ((w_in_hbm, w_in_ref), (pool_w_hbm, pool_w_ref),
                         (w_out_a_hbm, w_out_a_ref), (w_out_b_hbm, w_out_b_ref),
                         (w_o_hbm, w_o_ref)):
            jobs += _block_jobs(src, _pack_into(dst))
        _stream_blocks(jobs, stage_ref, stage_sem)

    @pl.when(s == 0)
    def _():
        av_ref[0:POOL_HALO, :] = jnp.zeros((POOL_HALO, e), jnp.float32)
        u_ref[0:CONV_HALO, :] = jnp.zeros((CONV_HALO, e), jnp.float32)

    mod = mod_ref[pl.ds(pl.program_id(0), 1), :]
    shift = mod[:, 0:d]
    gain = norm_g_ref[...] * (1.0 + mod[:, d:2 * d])
    gate = mod[:, 2 * d:3 * d]

    def make_chunk(c):
        r0 = c * rc
        v = {}

        def tile_cols(h, k=0):
            return slice(k * e + h * COL_TILE, k * e + (h + 1) * COL_TILE)

        def proj(k, h):
            cols = tile_cols(h, k)
            return (jnp.dot(v["hb"], _bf16_rows(w_in_ref[:, cols]), preferred_element_type=jnp.float32)
                    + b_in_ref[:, cols])

        def wide(tile_fn):
            return jnp.concatenate([tile_fn(h) for h in range(e // COL_TILE)], axis=1)

        def front():
            x = x_ref[0, r0:r0 + rc, :]
            h = (x * _rms_scale(x)) * gain + shift
            v["hb"] = h.astype(jnp.bfloat16)

        def pool_value():
            for h in range(e // COL_TILE):
                av_ref[POOL_HALO:, tile_cols(h)] = proj(A_V, h)

        def conv_value():
            def tile(h):
                cols = tile_cols(h)
                u = proj(B_C, h) * proj(B_V, h)
                u_ref[CONV_HALO:, cols] = u
                u_ext = u_ref[:, cols]
                u_ref[0:CONV_HALO, cols] = u[rc - CONV_HALO:, :]
                conv = conv_b_ref[:, cols]
                for j in range(CONV_K):
                    lag = CONV_K - 1 - j
                    tap = u if lag == 0 else pltpu.roll(u_ext, lag, 0)[CONV_HALO:, :]
                    conv = conv + tap * conv_w_ref[j:j + 1, cols]
                return conv
            v["conv"] = [tile(h) for h in range(e // COL_TILE)]

        def pool_gate():
            v["silu_a"] = wide(lambda h: _silu(proj(A_G, h)))

        def pool_mix():
            a_ext = av_ref[...]
            av_ref[0:POOL_HALO, :] = a_ext[rc:, :]
            frames_seen = (s * ts + r0 + 1
                           + lax.broadcasted_iota(jnp.int32, (rc, LANES), 0)).astype(jnp.float32)
            mixed = []
            for gi, w in enumerate(POOL_WINDOWS):
                grp = a_ext[:, gi * pg:(gi + 1) * pg]
                wsum = grp
                sh = 1
                while sh < w:
                    wsum = wsum + pltpu.roll(wsum, sh, 0)
                    sh *= 2
                inv_cnt = 1.0 / jnp.minimum(frames_seen, float(w))
                inv_cnt = jnp.concatenate([inv_cnt] * (pg // LANES), axis=1)
                pooled = wsum[POOL_HALO:, :] * inv_cnt - grp[POOL_HALO:, :]
                group_w = _bf16_rows(pool_w_ref[gi * pg // 2:(gi + 1) * pg // 2, :])
                mixed.append(jnp.dot(pooled.astype(jnp.bfloat16), group_w,
                                     preferred_element_type=jnp.float32))
            mixed = jnp.concatenate(mixed, axis=1)
            v["y_a"] = (mixed * pool_scale_ref[...] * v["silu_a"]).astype(jnp.bfloat16)

        def conv_gates():
            v["y_b"] = wide(lambda h: (proj(B_B, h) * v["conv"][h]
                                       * _silu(proj(B_G, h))).astype(jnp.bfloat16))

        def merge_a():
            def tile(h):
                o_a = jnp.dot(v["y_a"], _bf16_rows(w_out_a_ref[:, tile_cols(h)]),
                              preferred_element_type=jnp.float32)
                return jax.nn.sigmoid(proj(M_A, h)) * o_a
            v["merged"] = [tile(h) for h in range(e // COL_TILE)]

        def merge_b():
            def tile(h):
                o_b = jnp.dot(v["y_b"], _bf16_rows(w_out_b_ref[:, tile_cols(h)]),
                              preferred_element_type=jnp.float32)
                return (v["merged"][h] + jax.nn.sigmoid(proj(M_B, h)) * o_b).astype(jnp.bfloat16)
            v["merged"] = wide(tile)

        def output():
            out = wide(lambda h: jnp.dot(v["merged"], _bf16_rows(w_o_ref[:, tile_cols(h)]),
                                         preferred_element_type=jnp.float32))
            x_new = x_ref[0, r0:r0 + rc, :] + gate * out
            if final_norm:
                x_new = (x_new * _rms_scale(x_new)) * final_g_ref[...]
            o_ref[0, r0:r0 + rc, :] = x_new

        head = [front, pool_value, conv_value]
        body = [pool_gate, pool_mix, conv_gates, merge_a, merge_b]
        return head, body, output

    pending_output = None
    for head, body, output in [make_chunk(c) for c in range(ts // rc)]:
        for stage in head:
            stage()
        if pending_output is not None:
            pending_output()
        for stage in body:
            stage()
        pending_output = output
    pending_output()


def _resident(shape):
    return pl.BlockSpec(shape, lambda b, s: (0,) * len(shape), pipeline_mode=pl.Buffered(1))


def _block_layer(x, c, ada_w, ada_b, norm_g, b_in, pool_scale, conv_w, conv_b, final_g,
                 w_in, pool_w, w_out_a, w_out_b, w_o, *, final_norm):
    b, seq, d = x.shape
    e = w_out_a.shape[1]
    ts = SEQ_TILE
    assert seq % ts == 0 and ts % ROW_CHUNK == 0 and ROW_CHUNK >= POOL_HALO >= max(POOL_WINDOWS) - 1
    assert CONV_HALO >= CONV_K - 1
    assert w_in.shape == (d, 8 * e) and pool_w.shape[0] == len(POOL_WINDOWS)
    assert ada_w.shape == (d, 3 * d)
    row = lambda v: v.reshape(1, -1)
    small = [c, row(ada_b), row(norm_g), row(b_in), row(pool_scale), conv_w, row(conv_b),
             row(final_g)]
    g, pg, _ = pool_w.shape
    weights = [w_in, pool_w.reshape(g * pg, pg), w_out_a, w_out_b, w_o]
    assert all(w.shape[0] % PACK_ROWS == 0 and w.shape[1] % min(w.shape[1], PACK_COLS) == 0
               for w in [ada_w] + weights)
    packed = [pltpu.VMEM((w.shape[0] // 2, w.shape[1]), jnp.uint32) for w in weights]
    return pl.pallas_call(
        functools.partial(_block_kernel, final_norm=final_norm),
        grid=(b, seq // ts),
        in_specs=[pl.BlockSpec((1, ts, d), lambda i, s: (i, s, 0))]
                 + [_resident(v.shape) for v in small]
                 + [pl.BlockSpec(memory_space=pl.ANY) for _ in [ada_w] + weights],
        out_specs=pl.BlockSpec((1, ts, d), lambda i, s: (i, s, 0)),
        out_shape=jax.ShapeDtypeStruct(x.shape, x.dtype),
        scratch_shapes=[pltpu.VMEM((POOL_HALO + ROW_CHUNK, e), jnp.float32),
                        pltpu.VMEM((CONV_HALO + ROW_CHUNK, e), jnp.float32),
                        pltpu.VMEM((b, 3 * d), jnp.float32)]
                       + packed
                       + [pltpu.VMEM((STAGE_SLOTS, PACK_ROWS, PACK_COLS), jnp.float32),
                          pltpu.SemaphoreType.DMA((STAGE_SLOTS,))],
        compiler_params=pltpu.CompilerParams(
            dimension_semantics=("arbitrary", "arbitrary"),
            vmem_limit_bytes=VMEM_LIMIT_BYTES),
        name="fused_block",
    )(x, *small, ada_w, *weights)


def kernel(x, c, ada_w, ada_b, norm_g, w_in, b_in, pool_w, pool_scale, conv_w, conv_b,
           w_out_a, w_out_b, w_o, final_g):
    depth = ada_w.shape[0]
    for l in range(depth):
        x = _block_layer(x, c, ada_w[l], ada_b[l], norm_g[l], b_in[l], pool_scale[l], conv_w[l],
                         conv_b[l], final_g, w_in[l], pool_w[l], w_out_a[l], w_out_b[l], w_o[l],
                         final_norm=(l == depth - 1))
    return x
```

```python
import functools

import jax
import jax.numpy as jnp
from jax import lax
from jax.experimental import pallas as pl
from jax.experimental.pallas import tpu as pltpu

POOL_WINDOWS = (2, 4, 8, 16)
CONV_K = 3
RMS_EPS = 1e-6

SEQ_TILE = 512
ROW_CHUNK = 256
COL_TILE = 256
POOL_HALO = 16
CONV_HALO = 8
LANES = 128
VMEM_LIMIT_BYTES = 52 * 1024 * 1024
PACK_ROWS, PACK_COLS = 512, 1024
STAGE_SLOTS = 4
N_SMALL = 8

A_V, A_G, B_B, B_C, B_V, B_G, M_A, M_B = range(8)


def _silu(v):
    return v * jax.nn.sigmoid(v)


def _block_jobs(src, sink):
    rows, width = src.shape
    cols = min(width, PACK_COLS)
    return [(src, row0, col0, cols, sink)
            for col0 in range(0, width, cols) for row0 in range(0, rows, PACK_ROWS)]


def _stream_blocks(jobs, stage_ref, sem):
    n_slots = stage_ref.shape[0]

    def copy(i):
        src, row0, col0, cols, _ = jobs[i]
        slot = i % n_slots
        return pltpu.make_async_copy(src.at[pl.ds(row0, PACK_ROWS), pl.ds(col0, cols)],
                                     stage_ref.at[slot, :, pl.ds(0, cols)], sem.at[slot])

    for i in range(min(n_slots - 1, len(jobs))):
        copy(i).start()
    for i, (_, row0, col0, cols, sink) in enumerate(jobs):
        if i + n_slots - 1 < len(jobs):
            copy(i + n_slots - 1).start()
        copy(i).wait()
        sink(row0, col0, stage_ref[i % n_slots, :, 0:cols].astype(jnp.bfloat16))


def _pack_into(dst):
    def sink(row0, col0, block):
        rows, cols = block.shape
        dst[row0 // 2:(row0 + rows) // 2, col0:col0 + cols] = pltpu.bitcast(block, jnp.uint32)
    return sink


def _bf16_rows(packed):
    return pltpu.bitcast(packed, jnp.bfloat16)


def _rms_scale(v):
    return lax.rsqrt(jnp.mean(v * v, axis=-1, keepdims=True) + RMS_EPS)


def _block_kernel(x_ref, *refs, final_norm):
    small_hbm, refs = refs[:N_SMALL], refs[N_SMALL:]
    ada_w_hbm, w_in_hbm, pool_w_hbm, w_out_a_hbm, w_out_b_hbm, w_o_hbm, o_ref = refs[:7]
    av_ref, u_ref, mod_ref = refs[7:10]
    small_refs = refs[10:10 + N_SMALL]
    (c_ref, ada_b_ref, norm_g_ref, b_in_ref, pool_scale_ref, conv_w_ref, conv_b_ref,
     final_g_ref) = small_refs
    (w_in_ref, pool_w_ref, w_out_a_ref, w_out_b_ref, w_o_ref,
     stage_ref, stage_sem, small_sem) = refs[10 + N_SMALL:]
    ts = x_ref.shape[1]
    d = x_ref.shape[2]
    e = w_out_a_ref.shape[1]
    pg = pool_w_ref.shape[1]
    rc = ROW_CHUNK
    s = pl.program_id(1)

    @pl.when(jnp.logical_and(pl.program_id(0) == 0, s == 0))
    def _():
        small_copies = [pltpu.make_async_copy(src, dst, small_sem.at[k])
                        for k, (src, dst) in enumerate(zip(small_hbm, small_refs))]
        for cp in small_copies:
            cp.start()
        started = {}

        def add_modulation(row0, col0, block):
            if not started:
                for cp in small_copies:
                    cp.wait()
                started["c_act"] = _silu(c_ref[...]).astype(jnp.bfloat16)
                mod_ref[...] = jnp.broadcast_to(ada_b_ref[...], mod_ref.shape)
            rows, cols = block.shape
            mod_ref[:, col0:col0 + cols] += jnp.dot(started["c_act"][:, row0:row0 + rows], block,
                                                    preferred_element_type=jnp.float32)

        jobs = []
        for src, dst in ((w_in_hbm, w_in_ref), (pool_w_hbm, pool_w_ref),
                         (w_out_a_hbm, w_out_a_ref), (w_out_b_hbm, w_out_b_ref),
                         (w_o_hbm, w_o_ref)):
            jobs += _block_jobs(src, _pack_into(dst))
        jobs += _block_jobs(ada_w_hbm, add_modulation)
        _stream_blocks(jobs, stage_ref, stage_sem)

    @pl.when(s == 0)
    def _():
        av_ref[0:POOL_HALO, :] = jnp.zeros((POOL_HALO, e), jnp.float32)
        u_ref[0:CONV_HALO, :] = jnp.zeros((CONV_HALO, e), jnp.float32)

    mod = mod_ref[pl.ds(pl.program_id(0), 1), :]
    shift = mod[:, 0:d]
    gain = norm_g_ref[...] * (1.0 + mod[:, d:2 * d])
    gate = mod[:, 2 * d:3 * d]

    def make_chunk(c):
        r0 = c * rc
        v = {}

        def tile_cols(h, k=0):
            return slice(k * e + h * COL_TILE, k * e + (h + 1) * COL_TILE)

        def proj(k, h):
            cols = tile_cols(h, k)
            return (jnp.dot(v["hb"], _bf16_rows(w_in_ref[:, cols]), preferred_element_type=jnp.float32)
                    + b_in_ref[:, cols])

        def wide(tile_fn):
            return jnp.concatenate([tile_fn(h) for h in range(e // COL_TILE)], axis=1)

        def front():
            x = x_ref[0, r0:r0 + rc, :]
            h = (x * _rms_scale(x)) * gain + shift
            v["hb"] = h.astype(jnp.bfloat16)

        def pool_value():
            for h in range(e // COL_TILE):
                av_ref[POOL_HALO:, tile_cols(h)] = proj(A_V, h)

        def conv_value():
            def tile(h):
                cols = tile_cols(h)
                u = proj(B_C, h) * proj(B_V, h)
                u_ref[CONV_HALO:, cols] = u
                u_ext = u_ref[:, cols]
                u_ref[0:CONV_HALO, cols] = u[rc - CONV_HALO:, :]
                conv = conv_b_ref[:, cols]
                for j in range(CONV_K):
                    lag = CONV_K - 1 - j
                    tap = u if lag == 0 else pltpu.roll(u_ext, lag, 0)[CONV_HALO:, :]
                    conv = conv + tap * conv_w_ref[j:j + 1, cols]
                return conv
            v["conv"] = [tile(h) for h in range(e // COL_TILE)]

        def pool_gate():
            v["silu_a"] = wide(lambda h: _silu(proj(A_G, h)))

        def pool_mix():
            a_ext = av_ref[...]
            av_ref[0:POOL_HALO, :] = a_ext[rc:, :]
            frames_seen = (s * ts + r0 + 1
                           + lax.broadcasted_iota(jnp.int32, (rc, LANES), 0)).astype(jnp.float32)
            mixed = []
            for gi, w in enumerate(POOL_WINDOWS):
                grp = a_ext[:, gi * pg:(gi + 1) * pg]
                wsum = grp
                sh = 1
                while sh < w:
                    wsum = wsum + pltpu.roll(wsum, sh, 0)
                    sh *= 2
                inv_cnt = 1.0 / jnp.minimum(frames_seen, float(w))
                inv_cnt = jnp.concatenate([inv_cnt] * (pg // LANES), axis=1)
                pooled = wsum[POOL_HALO:, :] * inv_cnt - grp[POOL_HALO:, :]
                group_w = _bf16_rows(pool_w_ref[gi * pg // 2:(gi + 1) * pg // 2, :])
                mixed.append(jnp.dot(pooled.astype(jnp.bfloat16), group_w,
                                     preferred_element_type=jnp.float32))
            mixed = jnp.concatenate(mixed, axis=1)
            v["y_a"] = (mixed * pool_scale_ref[...] * v["silu_a"]).astype(jnp.bfloat16)

        def conv_gates():
            v["y_b"] = wide(lambda h: (proj(B_B, h) * v["conv"][h]
                                       * _silu(proj(B_G, h))).astype(jnp.bfloat16))

        def merge_a():
            def tile(h):
                o_a = jnp.dot(v["y_a"], _bf16_rows(w_out_a_ref[:, tile_cols(h)]),
                              preferred_element_type=jnp.float32)
                return jax.nn.sigmoid(proj(M_A, h)) * o_a
            v["merged"] = [tile(h) for h in range(e // COL_TILE)]

        def merge_b():
            def tile(h):
                o_b = jnp.dot(v["y_b"], _bf16_rows(w_out_b_ref[:, tile_cols(h)]),
                              preferred_element_type=jnp.float32)
                return (v["merged"][h] + jax.nn.sigmoid(proj(M_B, h)) * o_b).astype(jnp.bfloat16)
            v["merged"] = wide(tile)

        def output():
            out = wide(lambda h: jnp.dot(v["merged"], _bf16_rows(w_o_ref[:, tile_cols(h)]),
                                         preferred_element_type=jnp.float32))
            x_new = x_ref[0, r0:r0 + rc, :] + gate * out
            if final_norm:
                x_new = (x_new * _rms_scale(x_new)) * final_g_ref[...]
            o_ref[0, r0:r0 + rc, :] = x_new

        head = [front, pool_value, conv_value]
        body = [pool_gate, pool_mix, conv_gates, merge_a, merge_b]
        return head, body, output

    pending_output = None
    for head, body, output in [make_chunk(c) for c in range(ts // rc)]:
        for stage in head:
            stage()
        if pending_output is not None:
            pending_output()
        for stage in body:
            stage()
        pending_output = output
    pending_output()


def _block_layer(x, c, ada_w, ada_b, norm_g, b_in, pool_scale, conv_w, conv_b, final_g,
                 w_in, pool_w, w_out_a, w_out_b, w_o, *, final_norm):
    b, seq, d = x.shape
    e = w_out_a.shape[1]
    ts = SEQ_TILE
    assert seq % ts == 0 and ts % ROW_CHUNK == 0 and ROW_CHUNK >= POOL_HALO >= max(POOL_WINDOWS) - 1
    assert CONV_HALO >= CONV_K - 1
    assert w_in.shape == (d, 8 * e) and pool_w.shape[0] == len(POOL_WINDOWS)
    assert ada_w.shape == (d, 3 * d)
    row = lambda v: v.reshape(1, -1)
    small = [c, row(ada_b), row(norm_g), row(b_in), row(pool_scale), conv_w, row(conv_b),
             row(final_g)]
    assert len(small) == N_SMALL
    g, pg, _ = pool_w.shape
    weights = [w_in, pool_w.reshape(g * pg, pg), w_out_a, w_out_b, w_o]
    assert all(w.shape[0] % PACK_ROWS == 0 and w.shape[1] % min(w.shape[1], PACK_COLS) == 0
               for w in [ada_w] + weights)
    packed = [pltpu.VMEM((w.shape[0] // 2, w.shape[1]), jnp.uint32) for w in weights]
    return pl.pallas_call(
        functools.partial(_block_kernel, final_norm=final_norm),
        grid=(b, seq // ts),
        in_specs=[pl.BlockSpec((1, ts, d), lambda i, s: (i, s, 0))]
                 + [pl.BlockSpec(memory_space=pl.ANY) for _ in small + [ada_w] + weights],
        out_specs=pl.BlockSpec((1, ts, d), lambda i, s: (i, s, 0)),
        out_shape=jax.ShapeDtypeStruct(x.shape, x.dtype),
        scratch_shapes=[pltpu.VMEM((POOL_HALO + ROW_CHUNK, e), jnp.float32),
                        pltpu.VMEM((CONV_HALO + ROW_CHUNK, e), jnp.float32),
                        pltpu.VMEM((b, 3 * d), jnp.float32)]
                       + [pltpu.VMEM(v.shape, v.dtype) for v in small]
                       + packed
                       + [pltpu.VMEM((STAGE_SLOTS, PACK_ROWS, PACK_COLS), jnp.float32),
                          pltpu.SemaphoreType.DMA((STAGE_SLOTS,)),
                          pltpu.SemaphoreType.DMA((N_SMALL,))],
        compiler_params=pltpu.CompilerParams(
            dimension_semantics=("arbitrary", "arbitrary"),
            vmem_limit_bytes=VMEM_LIMIT_BYTES),
        name="fused_block",
    )(x, *small, ada_w, *weights)


def kernel(x, c, ada_w, ada_b, norm_g, w_in, b_in, pool_w, pool_scale, conv_w, conv_b,
           w_out_a, w_out_b, w_o, final_g):
    depth = ada_w.shape[0]
    for l in range(depth):
        x = _block_layer(x, c, ada_w[l], ada_b[l], norm_g[l], b_in[l], pool_scale[l], conv_w[l],
                         conv_b[l], final_g, w_in[l], pool_w[l], w_out_a[l], w_out_b[l], w_o[l],
                         final_norm=(l == depth - 1))
    return x
```

```python
import functools

import jax
import jax.numpy as jnp
from jax import lax
from jax.experimental import pallas as pl
from jax.experimental.pallas import tpu as pltpu

POOL_WINDOWS = (2, 4, 8, 16)
CONV_K = 3
RMS_EPS = 1e-6

SEQ_TILE = 512
ROW_CHUNK = 256
COL_TILE = 256
POOL_HALO = 16
CONV_HALO = 8
LANES = 128
VMEM_LIMIT_BYTES = 63 * 1024 * 1024
PACK_ROWS, PACK_COLS = 512, 1024
STAGE_SLOTS = 4
N_SMALL = 8

A_V, A_G, B_B, B_C, B_V, B_G, M_A, M_B = range(8)


def _silu(v):
    return v * jax.nn.sigmoid(v)


def _block_jobs(src, sink):
    rows, width = src.shape
    cols = min(width, PACK_COLS)
    return [(src, row0, col0, cols, sink)
            for col0 in range(0, width, cols) for row0 in range(0, rows, PACK_ROWS)]


def _stream_blocks(jobs, stage_ref, sem):
    n_slots = stage_ref.shape[0]

    def copy(i):
        src, row0, col0, cols, _ = jobs[i]
        slot = i % n_slots
        return pltpu.make_async_copy(src.at[pl.ds(row0, PACK_ROWS), pl.ds(col0, cols)],
                                     stage_ref.at[slot, :, pl.ds(0, cols)], sem.at[slot])

    for i in range(min(n_slots - 1, len(jobs))):
        copy(i).start()
    for i, (_, row0, col0, cols, sink) in enumerate(jobs):
        if i + n_slots - 1 < len(jobs):
            copy(i + n_slots - 1).start()
        copy(i).wait()
        sink(row0, col0, stage_ref[i % n_slots, :, 0:cols].astype(jnp.bfloat16))


def _pack_into(dst):
    def sink(row0, col0, block):
        rows, cols = block.shape
        dst[row0 // 2:(row0 + rows) // 2, col0:col0 + cols] = pltpu.bitcast(block, jnp.uint32)
    return sink


def _bf16_rows(packed):
    return pltpu.bitcast(packed, jnp.bfloat16)


def _rms_scale(v):
    return lax.rsqrt(jnp.mean(v * v, axis=-1, keepdims=True) + RMS_EPS)


def _block_kernel(x_ref, *refs, final_norm):
    small_hbm, refs = refs[:N_SMALL], refs[N_SMALL:]
    ada_w_hbm, w_in_hbm, pool_w_hbm, w_out_a_hbm, w_out_b_hbm, w_o_hbm, o_ref = refs[:7]
    av_ref, u_ref, mod_ref = refs[7:10]
    small_refs = refs[10:10 + N_SMALL]
    (c_ref, ada_b_ref, norm_g_ref, b_in_ref, pool_scale_ref, conv_w_ref, conv_b_ref,
     final_g_ref) = small_refs
    (w_in_ref, pool_w_ref, w_out_a_ref, w_out_b_ref, w_o_ref,
     stage_ref, stage_sem, small_sem) = refs[10 + N_SMALL:]
    ts = x_ref.shape[1]
    d = x_ref.shape[2]
    e = w_out_a_ref.shape[1]
    pg = pool_w_ref.shape[1]
    rc = ROW_CHUNK
    s = pl.program_id(1)

    @pl.when(jnp.logical_and(pl.program_id(0) == 0, s == 0))
    def _():
        small_copies = [pltpu.make_async_copy(src, dst, small_sem.at[k])
                        for k, (src, dst) in enumerate(zip(small_hbm, small_refs))]
        for cp in small_copies:
            cp.start()
        started = {}

        def add_modulation(row0, col0, block):
            if not started:
                for cp in small_copies:
                    cp.wait()
                started["c_act"] = _silu(c_ref[...]).astype(jnp.bfloat16)
                mod_ref[...] = jnp.broadcast_to(ada_b_ref[...], mod_ref.shape)
            rows, cols = block.shape
            mod_ref[:, col0:col0 + cols] += jnp.dot(started["c_act"][:, row0:row0 + rows], block,
                                                    preferred_element_type=jnp.float32)

        jobs = []
        for src, dst in ((w_in_hbm, w_in_ref), (pool_w_hbm, pool_w_ref),
                         (w_out_a_hbm, w_out_a_ref), (w_out_b_hbm, w_out_b_ref),
                         (w_o_hbm, w_o_ref)):
            jobs += _block_jobs(src, _pack_into(dst))
        jobs += _block_jobs(ada_w_hbm, add_modulation)
        _stream_blocks(jobs, stage_ref, stage_sem)

    @pl.when(s == 0)
    def _():
        av_ref[0:POOL_HALO, :] = jnp.zeros((POOL_HALO, e), jnp.float32)
        u_ref[0:CONV_HALO, :] = jnp.zeros((CONV_HALO, e), jnp.float32)

    mod = mod_ref[pl.ds(pl.program_id(0), 1), :]
    shift = mod[:, 0:d]
    gain = norm_g_ref[...] * (1.0 + mod[:, d:2 * d])
    gate = mod[:, 2 * d:3 * d]

    def make_chunk(c):
        r0 = c * rc
        v = {}

        def tile_cols(h, k=0):
            return slice(k * e + h * COL_TILE, k * e + (h + 1) * COL_TILE)

        def proj(k, h):
            cols = tile_cols(h, k)
            return (jnp.dot(v["hb"], _bf16_rows(w_in_ref[:, cols]), preferred_element_type=jnp.float32)
                    + b_in_ref[:, cols])

        def wide(tile_fn):
            return jnp.concatenate([tile_fn(h) for h in range(e // COL_TILE)], axis=1)

        def front():
            x = x_ref[0, r0:r0 + rc, :]
            h = (x * _rms_scale(x)) * gain + shift
            v["hb"] = h.astype(jnp.bfloat16)

        def pool_value():
            for h in range(e // COL_TILE):
                av_ref[POOL_HALO:, tile_cols(h)] = proj(A_V, h)

        def conv_value():
            def tile(h):
                cols = tile_cols(h)
                u = proj(B_C, h) * proj(B_V, h)
                u_ref[CONV_HALO:, cols] = u
                u_ext = u_ref[:, cols]
                u_ref[0:CONV_HALO, cols] = u[rc - CONV_HALO:, :]
                conv = conv_b_ref[:, cols]
                for j in range(CONV_K):
                    lag = CONV_K - 1 - j
                    tap = u if lag == 0 else pltpu.roll(u_ext, lag, 0)[CONV_HALO:, :]
                    conv = conv + tap * conv_w_ref[j:j + 1, cols]
                return conv
            v["conv"] = [tile(h) for h in range(e // COL_TILE)]

        def pool_gate():
            v["silu_a"] = wide(lambda h: _silu(proj(A_G, h)))

        def pool_mix():
            a_ext = av_ref[...]
            av_ref[0:POOL_HALO, :] = a_ext[rc:, :]
            frames_seen = (s * ts + r0 + 1
                           + lax.broadcasted_iota(jnp.int32, (rc, LANES), 0)).astype(jnp.float32)
            mixed = []
            for gi, w in enumerate(POOL_WINDOWS):
                grp = a_ext[:, gi * pg:(gi + 1) * pg]
                wsum = grp
                sh = 1
                while sh < w:
                    wsum = wsum + pltpu.roll(wsum, sh, 0)
                    sh *= 2
                inv_cnt = 1.0 / jnp.minimum(frames_seen, float(w))
                inv_cnt = jnp.concatenate([inv_cnt] * (pg // LANES), axis=1)
                pooled = wsum[POOL_HALO:, :] * inv_cnt - grp[POOL_HALO:, :]
                group_w = _bf16_rows(pool_w_ref[gi * pg // 2:(gi + 1) * pg // 2, :])
                mixed.append(jnp.dot(pooled.astype(jnp.bfloat16), group_w,
                                     preferred_element_type=jnp.float32))
            mixed = jnp.concatenate(mixed, axis=1)
            v["y_a"] = (mixed * pool_scale_ref[...] * v["silu_a"]).astype(jnp.bfloat16)

        def conv_gates():
            v["y_b"] = wide(lambda h: (proj(B_B, h) * v["conv"][h]
                                       * _silu(proj(B_G, h))).astype(jnp.bfloat16))

        def merge_a():
            def tile(h):
                o_a = jnp.dot(v["y_a"], _bf16_rows(w_out_a_ref[:, tile_cols(h)]),
                              preferred_element_type=jnp.float32)
                return jax.nn.sigmoid(proj(M_A, h)) * o_a
            v["merged"] = [tile(h) for h in range(e // COL_TILE)]

        def merge_b():
            def tile(h):
                o_b = jnp.dot(v["y_b"], _bf16_rows(w_out_b_ref[:, tile_cols(h)]),
                              preferred_element_type=jnp.float32)
                return (v["merged"][h] + jax.nn.sigmoid(proj(M_B, h)) * o_b).astype(jnp.bfloat16)
            v["merged"] = wide(tile)

        def output():
            out = wide(lambda h: jnp.dot(v["merged"], _bf16_rows(w_o_ref[:, tile_cols(h)]),
                                         preferred_element_type=jnp.float32))
            x_new = x_ref[0, r0:r0 + rc, :] + gate * out
            if final_norm:
                x_new = (x_new * _rms_scale(x_new)) * final_g_ref[...]
            o_ref[0, r0:r0 + rc, :] = x_new

        head = [front, pool_value, conv_value]
        body = [pool_gate, pool_mix, conv_gates, merge_a, merge_b]
        return head, body, output

    pending_output = None
    for head, body, output in [make_chunk(c) for c in range(ts // rc)]:
        for stage in head:
            stage()
        if pending_output is not None:
            pending_output()
        for stage in body:
            stage()
        pending_output = output
    pending_output()


def _block_layer(x, c, ada_w, ada_b, norm_g, b_in, pool_scale, conv_w, conv_b, final_g,
                 w_in, pool_w, w_out_a, w_out_b, w_o, *, final_norm):
    b, seq, d = x.shape
    e = w_out_a.shape[1]
    ts = SEQ_TILE
    assert seq % ts == 0 and ts % ROW_CHUNK == 0 and ROW_CHUNK >= POOL_HALO >= max(POOL_WINDOWS) - 1
    assert CONV_HALO >= CONV_K - 1
    assert w_in.shape == (d, 8 * e) and pool_w.shape[0] == len(POOL_WINDOWS)
    assert ada_w.shape == (d, 3 * d)
    row = lambda v: v.reshape(1, -1)
    small = [c, row(ada_b), row(norm_g), row(b_in), row(pool_scale), conv_w, row(conv_b),
             row(final_g)]
    assert len(small) == N_SMALL
    g, pg, _ = pool_w.shape
    weights = [w_in, pool_w.reshape(g * pg, pg), w_out_a, w_out_b, w_o]
    assert all(w.shape[0] % PACK_ROWS == 0 and w.shape[1] % min(w.shape[1], PACK_COLS) == 0
               for w in [ada_w] + weights)
    packed = [pltpu.VMEM((w.shape[0] // 2, w.shape[1]), jnp.uint32) for w in weights]
    return pl.pallas_call(
        functools.partial(_block_kernel, final_norm=final_norm),
        grid=(b, seq // ts),
        in_specs=[pl.BlockSpec((1, ts, d), lambda i, s: (i, s, 0))]
                 + [pl.BlockSpec(memory_space=pl.ANY) for _ in small + [ada_w] + weights],
        out_specs=pl.BlockSpec((1, ts, d), lambda i, s: (i, s, 0)),
        out_shape=jax.ShapeDtypeStruct(x.shape, x.dtype),
        scratch_shapes=[pltpu.VMEM((POOL_HALO + ROW_CHUNK, e), jnp.float32),
                        pltpu.VMEM((CONV_HALO + ROW_CHUNK, e), jnp.float32),
                        pltpu.VMEM((b, 3 * d), jnp.float32)]
                       + [pltpu.VMEM(v.shape, v.dtype) for v in small]
                       + packed
                       + [pltpu.VMEM((STAGE_SLOTS, PACK_ROWS, PACK_COLS), jnp.float32),
                          pltpu.SemaphoreType.DMA((STAGE_SLOTS,)),
                          pltpu.SemaphoreType.DMA((N_SMALL,))],
        compiler_params=pltpu.CompilerParams(
            dimension_semantics=("arbitrary", "arbitrary"),
            vmem_limit_bytes=VMEM_LIMIT_BYTES),
        name="fused_block",
    )(x, *small, ada_w, *weights)


def kernel(x, c, ada_w, ada_b, norm_g, w_in, b_in, pool_w, pool_scale, conv_w, conv_b,
           w_out_a, w_out_b, w_o, final_g):
    depth = ada_w.shape[0]
    for l in range(depth):
        x = _block_layer(x, c, ada_w[l], ada_b[l], norm_g[l], b_in[l], pool_scale[l], conv_w[l],
                         conv_b[l], final_g, w_in[l], pool_w[l], w_out_a[l], w_out_b[l], w_o[l],
                         final_norm=(l == depth - 1))
    return x
```

```python
import functools

import jax
import jax.numpy as jnp
from jax import lax
from jax.experimental import pallas as pl
from jax.experimental.pallas import tpu as pltpu

POOL_WINDOWS = (2, 4, 8, 16)
CONV_K = 3
RMS_EPS = 1e-6

SEQ_TILE = 512
ROW_CHUNK = 256
COL_TILE = 256
POOL_HALO = 16
CONV_HALO = 8
LANES = 128
VMEM_LIMIT_BYTES = 63 * 1024 * 1024
PACK_ROWS, PACK_COLS = 512, 1024
STAGE_SLOTS = 4
N_SMALL = 8

A_V, A_G, B_B, B_C, B_V, B_G, M_A, M_B = range(8)


def _silu(v):
    return v * jax.nn.sigmoid(v)


def _block_jobs(src, sink):
    rows, width = src.shape
    cols = min(width, PACK_COLS)
    return [(src, row0, col0, cols, sink)
            for col0 in range(0, width, cols) for row0 in range(0, rows, PACK_ROWS)]


def _stream_blocks(jobs, stage_ref, sem):
    n_slots = stage_ref.shape[0]

    def copy(i):
        src, row0, col0, cols, _ = jobs[i]
        slot = i % n_slots
        return pltpu.make_async_copy(src.at[pl.ds(row0, PACK_ROWS), pl.ds(col0, cols)],
                                     stage_ref.at[slot, :, pl.ds(0, cols)], sem.at[slot])

    for i in range(min(n_slots - 1, len(jobs))):
        copy(i).start()
    for i, (_, row0, col0, cols, sink) in enumerate(jobs):
        if i + n_slots - 1 < len(jobs):
            copy(i + n_slots - 1).start()
        copy(i).wait()
        sink(row0, col0, stage_ref[i % n_slots, :, 0:cols].astype(jnp.bfloat16))


def _pack_into(dst):
    def sink(row0, col0, block):
        rows, cols = block.shape
        dst[row0 // 2:(row0 + rows) // 2, col0:col0 + cols] = pltpu.bitcast(block, jnp.uint32)
    return sink


def _bf16_rows(packed):
    return pltpu.bitcast(packed, jnp.bfloat16)


def _rms_scale(v):
    return lax.rsqrt(jnp.mean(v * v, axis=-1, keepdims=True) + RMS_EPS)


def _block_kernel(x_ref, x_next_ref, *refs, final_norm, tiles_per_seq):
    small_hbm, refs = refs[:N_SMALL], refs[N_SMALL:]
    ada_w_hbm, w_in_hbm, pool_w_hbm, w_out_a_hbm, w_out_b_hbm, w_o_hbm, o_ref = refs[:7]
    av_ref, u_ref, mod_ref, hb_ref, conv_ref, pooled_ref = refs[7:13]
    small_refs = refs[13:13 + N_SMALL]
    (c_ref, ada_b_ref, norm_g_ref, b_in_ref, pool_scale_ref, conv_w_ref, conv_b_ref,
     final_g_ref) = small_refs
    (w_in_ref, pool_w_ref, w_out_a_ref, w_out_b_ref, w_o_ref,
     stage_ref, stage_sem, small_sem) = refs[13 + N_SMALL:]
    d = x_ref.shape[2]
    e = w_out_a_ref.shape[1]
    pg = pool_w_ref.shape[1]
    rc = ROW_CHUNK
    n_col_tiles = e // COL_TILE
    t = pl.program_id(0)
    t_next = jnp.minimum(t + 1, pl.num_programs(0) - 1)
    tps = jnp.int32(tiles_per_seq)
    s = lax.rem(t, tps)
    s_next = lax.rem(t_next, tps)

    def tile_cols(h, k=0):
        return slice(k * e + h * COL_TILE, k * e + (h + 1) * COL_TILE)

    def wide(tile_fn):
        return jnp.concatenate([tile_fn(h) for h in range(n_col_tiles)], axis=1)

    def mod_row(tile):
        return mod_ref[pl.ds(lax.div(tile, tps), 1), :]

    def hidden(x, mod):
        gain = norm_g_ref[...] * (1.0 + mod[:, d:2 * d])
        return ((x * _rms_scale(x)) * gain + mod[:, 0:d]).astype(jnp.bfloat16)

    def proj(v, k, h):
        cols = tile_cols(h, k)
        return (jnp.dot(v["hb"], _bf16_rows(w_in_ref[:, cols]), preferred_element_type=jnp.float32)
                + b_in_ref[:, cols])

    def pool_value(v):
        for h in range(n_col_tiles):
            av_ref[POOL_HALO:, tile_cols(h)] = proj(v, A_V, h)

    def conv_value(v):
        def tile(h):
            cols = tile_cols(h)
            u = proj(v, B_C, h) * proj(v, B_V, h)
            u_ref[CONV_HALO:, cols] = u
            u_ext = u_ref[:, cols]
            u_ref[0:CONV_HALO, cols] = u[rc - CONV_HALO:, :]
            conv = conv_b_ref[:, cols]
            for j in range(CONV_K):
                lag = CONV_K - 1 - j
                tap = u if lag == 0 else pltpu.roll(u_ext, lag, 0)[CONV_HALO:, :]
                conv = conv + tap * conv_w_ref[j:j + 1, cols]
            return conv
        v["conv"] = [tile(h) for h in range(n_col_tiles)]

    def pool_gate(v):
        v["silu_a"] = wide(lambda h: _silu(proj(v, A_G, h)))

    def pool_residual(v, first_frame):
        a_ext = av_ref[...]
        av_ref[0:POOL_HALO, :] = a_ext[rc:, :]
        frames_seen = (first_frame + 1
                       + lax.broadcasted_iota(jnp.int32, (rc, LANES), 0)).astype(jnp.float32)
        v["pooled"] = []
        for gi, w in enumerate(POOL_WINDOWS):
            grp = a_ext[:, gi * pg:(gi + 1) * pg]
            wsum = grp
            sh = 1
            while sh < w:
                wsum = wsum + pltpu.roll(wsum, sh, 0)
                sh *= 2
            inv_cnt = 1.0 / jnp.minimum(frames_seen, float(w))
            inv_cnt = jnp.concatenate([inv_cnt] * (pg // LANES), axis=1)
            pooled = wsum[POOL_HALO:, :] * inv_cnt - grp[POOL_HALO:, :]
            v["pooled"].append(pooled.astype(jnp.bfloat16))

    def pool_mix(v):
        mixed = []
        for gi, pooled in enumerate(v["pooled"]):
            group_w = _bf16_rows(pool_w_ref[gi * pg // 2:(gi + 1) * pg // 2, :])
            mixed.append(jnp.dot(pooled, group_w, preferred_element_type=jnp.float32))
        mixed = jnp.concatenate(mixed, axis=1)
        v["y_a"] = (mixed * pool_scale_ref[...] * v["silu_a"]).astype(jnp.bfloat16)

    def conv_gates(v):
        v["y_b"] = wide(lambda h: (proj(v, B_B, h) * v["conv"][h]
                                   * _silu(proj(v, B_G, h))).astype(jnp.bfloat16))

    def merge_a(v):
        def tile(h):
            o_a = jnp.dot(v["y_a"], _bf16_rows(w_out_a_ref[:, tile_cols(h)]),
                          preferred_element_type=jnp.float32)
            return jax.nn.sigmoid(proj(v, M_A, h)) * o_a
        v["merged"] = [tile(h) for h in range(n_col_tiles)]

    def merge_b(v):
        def tile(h):
            o_b = jnp.dot(v["y_b"], _bf16_rows(w_out_b_ref[:, tile_cols(h)]),
                          preferred_element_type=jnp.float32)
            return (v["merged"][h] + jax.nn.sigmoid(proj(v, M_B, h)) * o_b).astype(jnp.bfloat16)
        v["merged"] = wide(tile)

    def body(v):
        pool_gate(v)
        pool_mix(v)
        conv_gates(v)
        merge_a(v)
        merge_b(v)

    def output(v, rows, mod):
        out = wide(lambda h: jnp.dot(v["merged"], _bf16_rows(w_o_ref[:, tile_cols(h)]),
                                     preferred_element_type=jnp.float32))
        x_new = x_ref[0, rows, :] + mod[:, 2 * d:3 * d] * out
        if final_norm:
            x_new = (x_new * _rms_scale(x_new)) * final_g_ref[...]
        o_ref[0, rows, :] = x_new

    def carry(v):
        hb_ref[...] = v["hb"]
        conv_ref[...] = jnp.concatenate(v["conv"], axis=1)
        pooled_ref[...] = jnp.concatenate(v["pooled"], axis=1)

    @pl.when(t == 0)
    def _():
        small_copies = [pltpu.make_async_copy(src, dst, small_sem.at[k])
                        for k, (src, dst) in enumerate(zip(small_hbm, small_refs))]
        for cp in small_copies:
            cp.start()
        started = {}

        def add_modulation(row0, col0, block):
            if not started:
                for cp in small_copies:
                    cp.wait()
                started["c_act"] = _silu(c_ref[...]).astype(jnp.bfloat16)
                mod_ref[...] = jnp.broadcast_to(ada_b_ref[...], mod_ref.shape)
            rows, cols = block.shape
            mod_ref[:, col0:col0 + cols] += jnp.dot(started["c_act"][:, row0:row0 + rows], block,
                                                    preferred_element_type=jnp.float32)

        jobs = []
        for src, dst in ((w_in_hbm, w_in_ref), (pool_w_hbm, pool_w_ref),
                         (w_out_a_hbm, w_out_a_ref), (w_out_b_hbm, w_out_b_ref),
                         (w_o_hbm, w_o_ref)):
            jobs += _block_jobs(src, _pack_into(dst))
        jobs += _block_jobs(ada_w_hbm, add_modulation)
        _stream_blocks(jobs, stage_ref, stage_sem)

        av_ref[0:POOL_HALO, :] = jnp.zeros((POOL_HALO, e), jnp.float32)
        u_ref[0:CONV_HALO, :] = jnp.zeros((CONV_HALO, e), jnp.float32)
        v0 = {"hb": hidden(x_ref[0, 0:rc, :], mod_row(t))}
        pool_value(v0)
        pool_residual(v0, 0)
        conv_value(v0)
        carry(v0)

    mod_cur = mod_row(t)
    mod_next = mod_row(t_next)
    first = {"hb": hb_ref[...],
             "conv": [conv_ref[:, tile_cols(h)] for h in range(n_col_tiles)],
             "pooled": [pooled_ref[:, gi * pg:(gi + 1) * pg] for gi in range(len(POOL_WINDOWS))]}
    second = {}
    upcoming = {}

    body(first)
    second["hb"] = hidden(x_ref[0, rc:2 * rc, :], mod_cur)
    pool_value(second)
    pool_residual(second, s * (2 * rc) + rc)
    conv_value(second)
    output(first, slice(0, rc), mod_cur)
    body(second)
    keep = s_next != 0
    av_ref[0:POOL_HALO, :] = jnp.where(keep, av_ref[0:POOL_HALO, :], 0.0)
    u_ref[0:CONV_HALO, :] = jnp.where(keep, u_ref[0:CONV_HALO, :], 0.0)
    upcoming["hb"] = hidden(x_next_ref[0], mod_next)
    pool_value(upcoming)
    output(second, slice(rc, 2 * rc), mod_cur)
    pool_residual(upcoming, s_next * (2 * rc))
    conv_value(upcoming)
    carry(upcoming)


def _block_layer(x, c, ada_w, ada_b, norm_g, b_in, pool_scale, conv_w, conv_b, final_g,
                 w_in, pool_w, w_out_a, w_out_b, w_o, *, final_norm):
    b, seq, d = x.shape
    e = w_out_a.shape[1]
    ts = SEQ_TILE
    assert seq % ts == 0 and ts % ROW_CHUNK == 0 and ROW_CHUNK >= POOL_HALO >= max(POOL_WINDOWS) - 1
    assert CONV_HALO >= CONV_K - 1
    assert w_in.shape == (d, 8 * e) and pool_w.shape[0] == len(POOL_WINDOWS)
    assert ada_w.shape == (d, 3 * d)
    row = lambda v: v.reshape(1, -1)
    small = [c, row(ada_b), row(norm_g), row(b_in), row(pool_scale), conv_w, row(conv_b),
             row(final_g)]
    assert len(small) == N_SMALL
    g, pg, _ = pool_w.shape
    weights = [w_in, pool_w.reshape(g * pg, pg), w_out_a, w_out_b, w_o]
    assert all(w.shape[0] % PACK_ROWS == 0 and w.shape[1] % min(w.shape[1], PACK_COLS) == 0
               for w in [ada_w] + weights)
    packed = [pltpu.VMEM((w.shape[0] // 2, w.shape[1]), jnp.uint32) for w in weights]
    rc = ROW_CHUNK
    assert ts == 2 * rc
    tps = seq // ts
    n_tiles = b * tps
    nxt = lambda t: jnp.minimum(t + 1, n_tiles - 1)
    tile_spec = pl.BlockSpec((1, ts, d), lambda t: (t // tps, t % tps, 0))
    return pl.pallas_call(
        functools.partial(_block_kernel, final_norm=final_norm, tiles_per_seq=tps),
        grid=(n_tiles,),
        in_specs=[tile_spec,
                  pl.BlockSpec((1, rc, d), lambda t: (nxt(t) // tps, (nxt(t) % tps) * (ts // rc), 0))]
                 + [pl.BlockSpec(memory_space=pl.ANY) for _ in small + [ada_w] + weights],
        out_specs=tile_spec,
        out_shape=jax.ShapeDtypeStruct(x.shape, x.dtype),
        scratch_shapes=[pltpu.VMEM((POOL_HALO + ROW_CHUNK, e), jnp.float32),
                        pltpu.VMEM((CONV_HALO + ROW_CHUNK, e), jnp.float32),
                        pltpu.VMEM((b, 3 * d), jnp.float32),
                        pltpu.VMEM((rc, d), jnp.bfloat16),
                        pltpu.VMEM((rc, e), jnp.float32),
                        pltpu.VMEM((rc, e), jnp.bfloat16)]
                       + [pltpu.VMEM(v.shape, v.dtype) for v in small]
                       + packed
                       + [pltpu.VMEM((STAGE_SLOTS, PACK_ROWS, PACK_COLS), jnp.float32),
                          pltpu.SemaphoreType.DMA((STAGE_SLOTS,)),
                          pltpu.SemaphoreType.DMA((N_SMALL,))],
        compiler_params=pltpu.CompilerParams(
            dimension_semantics=("arbitrary",),
            vmem_limit_bytes=VMEM_LIMIT_BYTES),
        name="fused_block",
    )(x, x, *small, ada_w, *weights)


def kernel(x, c, ada_w, ada_b, norm_g, w_in, b_in, pool_w, pool_scale, conv_w, conv_b,
           w_out_a, w_out_b, w_o, final_g):
    depth = ada_w.shape[0]
    for l in range(depth):
        x = _block_layer(x, c, ada_w[l], ada_b[l], norm_g[l], b_in[l], pool_scale[l], conv_w[l],
                         conv_b[l], final_g, w_in[l], pool_w[l], w_out_a[l], w_out_b[l], w_o[l],
                         final_norm=(l == depth - 1))
    return x
```

```python
import functools

import jax
import jax.numpy as jnp
from jax import lax
from jax.experimental import pallas as pl
from jax.experimental.pallas import tpu as pltpu

POOL_WINDOWS = (2, 4, 8, 16)
CONV_K = 3
RMS_EPS = 1e-6

SEQ_TILE = 512
ROW_CHUNK = 256
COL_TILE = 256
POOL_HALO = 16
CONV_HALO = 8
LANES = 128
VMEM_LIMIT_BYTES = 63 * 1024 * 1024
PACK_ROWS, PACK_COLS = 512, 1024
STAGE_SLOTS = 4
N_SMALL = 8

A_V, A_G, B_B, B_C, B_V, B_G, M_A, M_B = range(8)


def _silu(v):
    return v * jax.nn.sigmoid(v)


def _block_jobs(src, sink):
    rows, width = src.shape
    cols = min(width, PACK_COLS)
    return [(src, row0, col0, cols, sink)
            for col0 in range(0, width, cols) for row0 in range(0, rows, PACK_ROWS)]


def _stream_blocks(jobs, stage_ref, sem):
    n_slots = stage_ref.shape[0]

    def copy(i):
        src, row0, col0, cols, _ = jobs[i]
        slot = i % n_slots
        return pltpu.make_async_copy(src.at[pl.ds(row0, PACK_ROWS), pl.ds(col0, cols)],
                                     stage_ref.at[slot, :, pl.ds(0, cols)], sem.at[slot])

    for i in range(min(n_slots - 1, len(jobs))):
        copy(i).start()
    for i, (_, row0, col0, cols, sink) in enumerate(jobs):
        if i + n_slots - 1 < len(jobs):
            copy(i + n_slots - 1).start()
        copy(i).wait()
        sink(row0, col0, stage_ref[i % n_slots, :, 0:cols].astype(jnp.bfloat16))


def _pack_into(dst):
    def sink(row0, col0, block):
        rows, cols = block.shape
        dst[row0 // 2:(row0 + rows) // 2, col0:col0 + cols] = pltpu.bitcast(block, jnp.uint32)
    return sink


def _bf16_rows(packed):
    return pltpu.bitcast(packed, jnp.bfloat16)


def _rms_scale(v):
    return lax.rsqrt(jnp.mean(v * v, axis=-1, keepdims=True) + RMS_EPS)


def _block_kernel(x_ref, *refs, final_norm):
    small_hbm, refs = refs[:N_SMALL], refs[N_SMALL:]
    ada_w_hbm, w_in_hbm, pool_w_hbm, w_out_a_hbm, w_out_b_hbm, w_o_hbm, o_ref = refs[:7]
    av_ref, u_ref, mod_ref = refs[7:10]
    small_refs = refs[10:10 + N_SMALL]
    (c_ref, ada_b_ref, norm_g_ref, b_in_ref, pool_scale_ref, conv_w_ref, conv_b_ref,
     final_g_ref) = small_refs
    (w_in_ref, pool_w_ref, w_out_a_ref, w_out_b_ref, w_o_ref,
     stage_ref, stage_sem, small_sem) = refs[10 + N_SMALL:]
    ts = x_ref.shape[1]
    d = x_ref.shape[2]
    e = w_out_a_ref.shape[1]
    pg = pool_w_ref.shape[1]
    rc = ROW_CHUNK
    s = pl.program_id(1)

    @pl.when(jnp.logical_and(pl.program_id(0) == 0, s == 0))
    def _():
        small_copies = [pltpu.make_async_copy(src, dst, small_sem.at[k])
                        for k, (src, dst) in enumerate(zip(small_hbm, small_refs))]
        for cp in small_copies:
            cp.start()
        started = {}

        def add_modulation(row0, col0, block):
            if not started:
                for cp in small_copies:
                    cp.wait()
                started["c_act"] = _silu(c_ref[...]).astype(jnp.bfloat16)
                mod_ref[...] = jnp.broadcast_to(ada_b_ref[...], mod_ref.shape)
            rows, cols = block.shape
            mod_ref[:, col0:col0 + cols] += jnp.dot(started["c_act"][:, row0:row0 + rows], block,
                                                    preferred_element_type=jnp.float32)

        jobs = []
        for src, dst in ((w_in_hbm, w_in_ref), (pool_w_hbm, pool_w_ref),
                         (w_out_a_hbm, w_out_a_ref), (w_out_b_hbm, w_out_b_ref),
                         (w_o_hbm, w_o_ref)):
            jobs += _block_jobs(src, _pack_into(dst))
        jobs += _block_jobs(ada_w_hbm, add_modulation)
        _stream_blocks(jobs, stage_ref, stage_sem)

    @pl.when(s == 0)
    def _():
        av_ref[0:POOL_HALO, :] = jnp.zeros((POOL_HALO, e), jnp.float32)
        u_ref[0:CONV_HALO, :] = jnp.zeros((CONV_HALO, e), jnp.float32)

    mod = mod_ref[pl.ds(pl.program_id(0), 1), :]
    shift = mod[:, 0:d]
    gain = norm_g_ref[...] * (1.0 + mod[:, d:2 * d])
    gate = mod[:, 2 * d:3 * d]

    def make_chunk(c):
        r0 = c * rc
        v = {}

        def tile_cols(h, k=0):
            return slice(k * e + h * COL_TILE, k * e + (h + 1) * COL_TILE)

        def proj(k, h):
            cols = tile_cols(h, k)
            return (jnp.dot(v["hb"], _bf16_rows(w_in_ref[:, cols]), preferred_element_type=jnp.float32)
                    + b_in_ref[:, cols])

        def wide(tile_fn):
            return jnp.concatenate([tile_fn(h) for h in range(e // COL_TILE)], axis=1)

        def front():
            x = x_ref[0, r0:r0 + rc, :]
            h = (x * _rms_scale(x)) * gain + shift
            v["hb"] = h.astype(jnp.bfloat16)

        def pool_value():
            for h in range(e // COL_TILE):
                av_ref[POOL_HALO:, tile_cols(h)] = proj(A_V, h)

        def conv_value():
            def tile(h):
                cols = tile_cols(h)
                u = proj(B_C, h) * proj(B_V, h)
                u_ref[CONV_HALO:, cols] = u
                u_ext = u_ref[:, cols]
                u_ref[0:CONV_HALO, cols] = u[rc - CONV_HALO:, :]
                conv = conv_b_ref[:, cols]
                for j in range(CONV_K):
                    lag = CONV_K - 1 - j
                    tap = u if lag == 0 else pltpu.roll(u_ext, lag, 0)[CONV_HALO:, :]
                    conv = conv + tap * conv_w_ref[j:j + 1, cols]
                return conv
            v["conv"] = [tile(h) for h in range(e // COL_TILE)]

        def pool_gate():
            v["silu_a"] = wide(lambda h: _silu(proj(A_G, h)))

        def pool_mix():
            a_ext = av_ref[...]
            av_ref[0:POOL_HALO, :] = a_ext[rc:, :]
            frames_seen = (s * ts + r0 + 1
                           + lax.broadcasted_iota(jnp.int32, (rc, LANES), 0)).astype(jnp.float32)
            mixed = []
            for gi, w in enumerate(POOL_WINDOWS):
                grp = a_ext[:, gi * pg:(gi + 1) * pg]
                wsum = grp
                sh = 1
                while sh < w:
                    wsum = wsum + pltpu.roll(wsum, sh, 0)
                    sh *= 2
                inv_cnt = 1.0 / jnp.minimum(frames_seen, float(w))
                inv_cnt = jnp.concatenate([inv_cnt] * (pg // LANES), axis=1)
                pooled = wsum[POOL_HALO:, :] * inv_cnt - av_ref[POOL_HALO:, gi * pg:(gi + 1) * pg]
                group_w = _bf16_rows(pool_w_ref[gi * pg // 2:(gi + 1) * pg // 2, :])
                mixed.append(jnp.dot(pooled.astype(jnp.bfloat16), group_w,
                                     preferred_element_type=jnp.float32))
            mixed = jnp.concatenate(mixed, axis=1)
            v["y_a"] = (mixed * pool_scale_ref[...] * v["silu_a"]).astype(jnp.bfloat16)

        def conv_gates():
            v["y_b"] = wide(lambda h: (proj(B_B, h) * v["conv"][h]
                                       * _silu(proj(B_G, h))).astype(jnp.bfloat16))

        def merge_a():
            def tile(h):
                o_a = jnp.dot(v["y_a"], _bf16_rows(w_out_a_ref[:, tile_cols(h)]),
                              preferred_element_type=jnp.float32)
                return jax.nn.sigmoid(proj(M_A, h)) * o_a
            v["merged"] = [tile(h) for h in range(e // COL_TILE)]

        def merge_b():
            def tile(h):
                o_b = jnp.dot(v["y_b"], _bf16_rows(w_out_b_ref[:, tile_cols(h)]),
                              preferred_element_type=jnp.float32)
                return (v["merged"][h] + jax.nn.sigmoid(proj(M_B, h)) * o_b).astype(jnp.bfloat16)
            v["merged"] = wide(tile)

        def output():
            out = wide(lambda h: jnp.dot(v["merged"], _bf16_rows(w_o_ref[:, tile_cols(h)]),
                                         preferred_element_type=jnp.float32))
            x_new = x_ref[0, r0:r0 + rc, :] + gate * out
            if final_norm:
                x_new = (x_new * _rms_scale(x_new)) * final_g_ref[...]
            o_ref[0, r0:r0 + rc, :] = x_new

        head = [front, conv_value, pool_value]
        body = [pool_gate, pool_mix, conv_gates, merge_a, merge_b]
        return head, body, output

    pending_output = None
    for head, body, output in [make_chunk(c) for c in range(ts // rc)]:
        for stage in head:
            stage()
        if pending_output is not None:
            pending_output()
        for stage in body:
            stage()
        pending_output = output
    pending_output()


def _block_layer(x, c, ada_w, ada_b, norm_g, b_in, pool_scale, conv_w, conv_b, final_g,
                 w_in, pool_w, w_out_a, w_out_b, w_o, *, final_norm):
    b, seq, d = x.shape
    e = w_out_a.shape[1]
    ts = SEQ_TILE
    assert seq % ts == 0 and ts % ROW_CHUNK == 0 and ROW_CHUNK >= POOL_HALO >= max(POOL_WINDOWS) - 1
    assert CONV_HALO >= CONV_K - 1
    assert w_in.shape == (d, 8 * e) and pool_w.shape[0] == len(POOL_WINDOWS)
    assert ada_w.shape == (d, 3 * d)
    row = lambda v: v.reshape(1, -1)
    small = [c, row(ada_b), row(norm_g), row(b_in), row(pool_scale), conv_w, row(conv_b),
             row(final_g)]
    assert len(small) == N_SMALL
    g, pg, _ = pool_w.shape
    weights = [w_in, pool_w.reshape(g * pg, pg), w_out_a, w_out_b, w_o]
    assert all(w.shape[0] % PACK_ROWS == 0 and w.shape[1] % min(w.shape[1], PACK_COLS) == 0
               for w in [ada_w] + weights)
    packed = [pltpu.VMEM((w.shape[0] // 2, w.shape[1]), jnp.uint32) for w in weights]
    return pl.pallas_call(
        functools.partial(_block_kernel, final_norm=final_norm),
        grid=(b, seq // ts),
        in_specs=[pl.BlockSpec((1, ts, d), lambda i, s: (i, s, 0))]
                 + [pl.BlockSpec(memory_space=pl.ANY) for _ in small + [ada_w] + weights],
        out_specs=pl.BlockSpec((1, ts, d), lambda i, s: (i, s, 0)),
        out_shape=jax.ShapeDtypeStruct(x.shape, x.dtype),
        scratch_shapes=[pltpu.VMEM((POOL_HALO + ROW_CHUNK, e), jnp.float32),
                        pltpu.VMEM((CONV_HALO + ROW_CHUNK, e), jnp.float32),
                        pltpu.VMEM((b, 3 * d), jnp.float32)]
                       + [pltpu.VMEM(v.shape, v.dtype) for v in small]
                       + packed
                       + [pltpu.VMEM((STAGE_SLOTS, PACK_ROWS, PACK_COLS), jnp.float32),
                          pltpu.SemaphoreType.DMA((STAGE_SLOTS,)),
                          pltpu.SemaphoreType.DMA((N_SMALL,))],
        compiler_params=pltpu.CompilerParams(
            dimension_semantics=("arbitrary", "arbitrary"),
            vmem_limit_bytes=VMEM_LIMIT_BYTES),
        name="fused_block",
    )(x, *small, ada_w, *weights)


def kernel(x, c, ada_w, ada_b, norm_g, w_in, b_in, pool_w, pool_scale, conv_w, conv_b,
           w_out_a, w_out_b, w_o, final_g):
    depth = ada_w.shape[0]
    for l in range(depth):
        x = _block_layer(x, c, ada_w[l], ada_b[l], norm_g[l], b_in[l], pool_scale[l], conv_w[l],
                         conv_b[l], final_g, w_in[l], pool_w[l], w_out_a[l], w_out_b[l], w_o[l],
                         final_norm=(l == depth - 1))
    return x
```
